```python
import jax
import jax.numpy as jnp
from jax import lax
import numpy as np

D_MODEL = 1024
BATCH = 8
SEQ = 2048
DEPTH = 4
DEC_BATCH = 128
DEC_SEQ = 8
PAST_LEN = 16384
PAGE_SIZE = 128

N_MIXERS = 2
N_CONV_LAYERS = (DEPTH + 1) // 2
N_RWKV_LAYERS = DEPTH // 2
CONV_WIDTH = 31
CONV_STATE = CONV_WIDTH - 1
RWKV_HEAD = 64
RWKV_HEADS = D_MODEL // RWKV_HEAD
DECAY_LORA = 64
AAA_LORA = 64
MV_LORA = 32
GATE_LORA = 160
LNX_EPS = 64e-5
MEM_LEN = 256
XA_HEADS = 4
XA_HEAD_DIM = D_MODEL // XA_HEADS
D_FF = 4 * D_MODEL
RMS_EPS = 1e-6
LN_EPS = 1e-5
L2_EPS = 1e-12

kernel_name = 'hybrid_conformer_rwkv7_memxattn_step'


def rms_norm(x, g):
    xf = x.astype(jnp.float32)
    y = xf * lax.rsqrt(jnp.mean(xf * xf, axis=-1, keepdims=True) + RMS_EPS)
    return (y * g.astype(jnp.float32)).astype(x.dtype)


def standardize(x, eps):
    xf = x.astype(jnp.float32)
    xc = xf - jnp.mean(xf, axis=-1, keepdims=True)
    return xc * lax.rsqrt(jnp.mean(xc * xc, axis=-1, keepdims=True) + eps)


def conformer_conv(h, conv0, p, ci):
    d = h.shape[-1]
    u = h @ p['cv_w_in'][ci] + p['cv_b_in'][ci]
    u = u[..., :d] * jax.nn.sigmoid(u[..., d:])
    up = jnp.concatenate([conv0.astype(u.dtype), u], axis=1)
    filt = p['cv_w_dw'][ci][:, None, :].astype(u.dtype)
    z = lax.conv_general_dilated(up, filt, window_strides=(1,), padding='VALID',
                                 dimension_numbers=('NWC', 'WIO', 'NWC'),
                                 feature_group_count=d)
    z = z + p['cv_b_dw'][ci]
    z = standardize(z, LN_EPS) * p['cv_ln_g'][ci] + p['cv_ln_b'][ci]
    z = jax.nn.silu(z).astype(h.dtype)
    out = z @ p['cv_w_out'][ci] + p['cv_b_out'][ci]
    return out, up[:, -CONV_STATE:]


def wkv7_scan(r, w, k, v, a, b, s0):
    def step(s, inp):
        rt, wt, kt, vt, at, bt = inp
        sa = jnp.einsum('bhvk,bhk->bhv', s, at)
        s = s * wt[:, :, None, :] + sa[..., None] * bt[:, :, None, :] + vt[..., None] * kt[:, :, None, :]
        yt = jnp.einsum('bhvk,bhk->bhv', s, rt)
        return s, yt
    seq = tuple(jnp.moveaxis(z, 1, 0) for z in (r, w, k, v, a, b))
    s, y = lax.scan(step, s0, seq)
    return jnp.moveaxis(y, 0, 1), s


def rwkv7_time_mix(h, shift0, s0, v_first, p, ri):
    bsz, t, d = h.shape
    nh, n = RWKV_HEADS, RWKV_HEAD
    f32 = jnp.float32
    prev = jnp.concatenate([shift0[:, None, :].astype(h.dtype), h[:, :-1]], axis=1)
    xx = prev - h
    mix = p['rw_mix'][ri]
    xr, xw, xk, xv, xa, xg = (h + xx * mix[i] for i in range(6))
    r = xr @ p['rw_w_r'][ri]
    k = xk @ p['rw_w_k'][ri]
    v = xv @ p['rw_w_v'][ri]
    w_pre = (p['rw_w0'][ri] + jnp.tanh(xw @ p['rw_w1'][ri]) @ p['rw_w2'][ri]).astype(f32)
    w_log = -jax.nn.softplus(-w_pre) - 0.5
    decay = jnp.exp(-jnp.exp(w_log))
    if v_first is None:
        v_first = v
    else:
        vi = ri - 1
        v = v + (v_first - v) * jax.nn.sigmoid(p['rw_v0'][vi] + (xv @ p['rw_v1'][vi]) @ p['rw_v2'][vi])
    a = jax.nn.sigmoid(p['rw_a0'][ri] + (xa @ p['rw_a1'][ri]) @ p['rw_a2'][ri])
    g = jax.nn.sigmoid(xg @ p['rw_g1'][ri]) @ p['rw_g2'][ri]
    kk = (k * p['rw_k_k'][ri]).reshape(bsz, t, nh, n).astype(f32)
    kk = kk / jnp.maximum(jnp.sqrt(jnp.sum(kk * kk, axis=-1, keepdims=True)), L2_EPS)
    k = k * (1 + (a - 1) * p['rw_k_a'][ri])
    rh = r.reshape(bsz, t, nh, n).astype(f32)
    kh = k.reshape(bsz, t, nh, n).astype(f32)
    vh = v.reshape(bsz, t, nh, n).astype(f32)
    ah = a.reshape(bsz, t, nh, n).astype(f32)
    y, s = wkv7_scan(rh, decay.reshape(bsz, t, nh, n), kh, vh, -kk, kk * ah, s0.astype(f32))
    y = standardize(y, LNX_EPS).reshape(bsz, t, d) * p['rw_lnx_g'][ri] + p['rw_lnx_b'][ri]
    bonus = jnp.sum(rh * kh * p['rw_r_k'][ri], axis=-1, keepdims=True) * vh
    y = (y + bonus.reshape(bsz, t, d)).astype(h.dtype)
    out = (y * g) @ p['rw_w_o'][ri]
    return out, h[:, -1], s, v_first


def cross_attend(h, w_q, mem_k, mem_v, w_o):
    bsz, t, d = h.shape
    q = (h @ w_q).reshape(bsz, t, XA_HEADS, XA_HEAD_DIM)
    sc = jnp.einsum('bthd,bmhd->bhtm', q, mem_k.astype(q.dtype)).astype(jnp.float32) * (XA_HEAD_DIM ** -0.5)
    pr = jax.nn.softmax(sc, axis=-1).astype(h.dtype)
    o = jnp.einsum('bhtm,bmhd->bthd', pr, mem_v.astype(h.dtype)).reshape(bsz, t, d)
    return o @ w_o


def sq_relu_mlp(h, w_up, w_down):
    u = jax.nn.relu(h @ w_up)
    return (u * u) @ w_down


def trunk(x, mem_k, mem_v, conv_state, shift_state, wkv_state, p):
    conv_new, shift_new, wkv_new = [], [], []
    v_first = None
    for layer in range(DEPTH):
        g = p['norm_g'][layer]
        h = rms_norm(x, g[0])
        if layer % N_MIXERS == 0:
            ci = layer // N_MIXERS
            out, cs = conformer_conv(h, conv_state[ci], p, ci)
            conv_new.append(cs)
        else:
            ri = layer // N_MIXERS
            out, sh, s, v_first = rwkv7_time_mix(h, shift_state[ri], wkv_state[ri], v_first, p, ri)
            shift_new.append(sh)
            wkv_new.append(s)
        x = x + out
        x = x + cross_attend(rms_norm(x, g[1]), p['xa_w_q'][layer], mem_k[layer], mem_v[layer], p['xa_w_o'][layer])
        x = x + sq_relu_mlp(rms_norm(x, g[2]), p['mlp_w_up'][layer], p['mlp_w_down'][layer])
    y = rms_norm(x, p['final_g'])
    return y, jnp.stack(conv_new), jnp.stack(shift_new), jnp.stack(wkv_new)


def setup_inputs(seed: int = 0) -> dict:
    key = jax.random.key(seed)
    keys = iter(jax.random.split(key, 64))
    f32 = jnp.float32
    D, NC, NR, H, N = D_MODEL, N_CONV_LAYERS, N_RWKV_LAYERS, RWKV_HEADS, RWKV_HEAD

    def normal(shape, scale):
        return jax.random.normal(next(keys), shape, f32) * scale

    def around(shape, center, scale):
        return center + normal(shape, scale)

    return {
        'x_prompt': normal((BATCH, SEQ, D), 1.0),
        'x_sample': normal((DEC_BATCH, DEC_SEQ, D), 1.0),
        'mem_prompt': normal((BATCH, MEM_LEN, D), 1.0),
        'cache_mem_k': normal((DEPTH, DEC_BATCH, MEM_LEN, XA_HEADS, XA_HEAD_DIM), 1.0),
        'cache_mem_v': normal((DEPTH, DEC_BATCH, MEM_LEN, XA_HEADS, XA_HEAD_DIM), 1.0),
        'state_conv': normal((NC, DEC_BATCH, CONV_STATE, D), 0.5),
        'state_shift': normal((NR, DEC_BATCH, D), 1.0),
        'state_wkv': normal((NR, DEC_BATCH, H, N, N), 0.3),
        'norm_g': around((DEPTH, 3, D), 1.0, 0.02),
        'final_g': around((D,), 1.0, 0.02),
        'cv_w_in': normal((NC, D, 2 * D), D ** -0.5),
        'cv_b_in': normal((NC, 2 * D), 0.02),
        'cv_w_dw': normal((NC, CONV_WIDTH, D), CONV_WIDTH ** -0.5),
        'cv_b_dw': normal((NC, D), 0.02),
        'cv_ln_g': around((NC, D), 1.0, 0.02),
        'cv_ln_b': normal((NC, D), 0.02),
        'cv_w_out': normal((NC, D, D), D ** -0.5),
        'cv_b_out': normal((NC, D), 0.02),
        'rw_mix': jax.random.uniform(next(keys), (NR, 6, D), f32),
        'rw_w_r': normal((NR, D, D), D ** -0.5),
        'rw_w_k': normal((NR, D, D), D ** -0.5),
        'rw_w_v': normal((NR, D, D), D ** -0.5),
        'rw_w_o': normal((NR, D, D), D ** -0.5),
        'rw_w0': normal((NR, D), 0.5),
        'rw_w1': normal((NR, D, DECAY_LORA), D ** -0.5),
        'rw_w2': normal((NR, DECAY_LORA, D), 0.5 * DECAY_LORA ** -0.5),
        'rw_a0': normal((NR, D), 0.1),
        'rw_a1': normal((NR, D, AAA_LORA), D ** -0.5),
        'rw_a2': normal((NR, AAA_LORA, D), 0.5 * AAA_LORA ** -0.5),
        'rw_v0': around((max(NR - 1, 0), D), 0.5, 0.1),
        'rw_v1': normal((max(NR - 1, 0), D, MV_LORA), D ** -0.5),
        'rw_v2': normal((max(NR - 1, 0), MV_LORA, D), 0.5 * MV_LORA ** -0.5),
        'rw_g1': normal((NR, D, GATE_LORA), D ** -0.5),
        'rw_g2': normal((NR, GATE_LORA, D), GATE_LORA ** -0.5),
        'rw_k_k': around((NR, D), 0.85, 0.05),
        'rw_k_a': around((NR, D), 1.0, 0.05),
        'rw_r_k': normal((NR, H, N), 0.1),
        'rw_lnx_g': around((NR, D), 1.0, 0.02),
        'rw_lnx_b': normal((NR, D), 0.02),
        'xa_w_q': normal((DEPTH, D, D), D ** -0.5),
        'xa_w_kv': normal((DEPTH, D, 2 * D), D ** -0.5),
        'xa_w_o': normal((DEPTH, D, D), D ** -0.5),
        'mlp_w_up': normal((DEPTH, D, D_FF), D ** -0.5),
        'mlp_w_down': normal((DEPTH, D_FF, D), D_FF ** -0.5),
    }


def reference(x_prompt, x_sample, mem_prompt, cache_mem_k, cache_mem_v, state_conv, state_shift, state_wkv,
              norm_g, final_g,
              cv_w_in, cv_b_in, cv_w_dw, cv_b_dw, cv_ln_g, cv_ln_b, cv_w_out, cv_b_out,
              rw_mix, rw_w_r, rw_w_k, rw_w_v, rw_w_o, rw_w0, rw_w1, rw_w2, rw_a0, rw_a1, rw_a2,
              rw_v0, rw_v1, rw_v2, rw_g1, rw_g2, rw_k_k, rw_k_a, rw_r_k, rw_lnx_g, rw_lnx_b,
              xa_w_q, xa_w_kv, xa_w_o, mlp_w_up, mlp_w_down):
    p = {
        'norm_g': norm_g, 'final_g': final_g,
        'cv_w_in': cv_w_in, 'cv_b_in': cv_b_in, 'cv_w_dw': cv_w_dw, 'cv_b_dw': cv_b_dw,
        'cv_ln_g': cv_ln_g, 'cv_ln_b': cv_ln_b, 'cv_w_out': cv_w_out, 'cv_b_out': cv_b_out,
        'rw_mix': rw_mix, 'rw_w_r': rw_w_r, 'rw_w_k': rw_w_k, 'rw_w_v': rw_w_v, 'rw_w_o': rw_w_o,
        'rw_w0': rw_w0, 'rw_w1': rw_w1, 'rw_w2': rw_w2, 'rw_a0': rw_a0, 'rw_a1': rw_a1, 'rw_a2': rw_a2,
        'rw_v0': rw_v0, 'rw_v1': rw_v1, 'rw_v2': rw_v2, 'rw_g1': rw_g1, 'rw_g2': rw_g2,
        'rw_k_k': rw_k_k, 'rw_k_a': rw_k_a, 'rw_r_k': rw_r_k, 'rw_lnx_g': rw_lnx_g, 'rw_lnx_b': rw_lnx_b,
        'xa_w_q': xa_w_q, 'xa_w_o': xa_w_o, 'mlp_w_up': mlp_w_up, 'mlp_w_down': mlp_w_down,
    }
    bp, mlen = mem_prompt.shape[0], mem_prompt.shape[1]
    mem_kv = jnp.einsum('bmd,lde->lbme', mem_prompt, xa_w_kv)
    mem_k_prompt = mem_kv[..., :D_MODEL].reshape(DEPTH, bp, mlen, XA_HEADS, XA_HEAD_DIM)
    mem_v_prompt = mem_kv[..., D_MODEL:].reshape(DEPTH, bp, mlen, XA_HEADS, XA_HEAD_DIM)
    conv0 = jnp.zeros((N_CONV_LAYERS, bp, CONV_STATE, D_MODEL), x_prompt.dtype)
    shift0 = jnp.zeros((N_RWKV_LAYERS, bp, D_MODEL), x_prompt.dtype)
    wkv0 = jnp.zeros((N_RWKV_LAYERS, bp, RWKV_HEADS, RWKV_HEAD, RWKV_HEAD), jnp.float32)
    y_prompt, conv_prompt, shift_prompt, wkv_prompt = trunk(
        x_prompt, mem_k_prompt, mem_v_prompt, conv0, shift0, wkv0, p)
    y_sample, conv_sample, shift_sample, wkv_sample = trunk(
        x_sample, cache_mem_k, cache_mem_v, state_conv, state_shift, state_wkv, p)
    return (y_prompt, y_sample, mem_k_prompt, mem_v_prompt, conv_prompt, shift_prompt, wkv_prompt,
            conv_sample, shift_sample, wkv_sample)
```

```python
import functools
import math

import jax
import jax.numpy as jnp
from jax import lax
from jax.experimental import pallas as pl
from jax.experimental.pallas import tpu as pltpu

F32 = jnp.float32
BF16 = jnp.bfloat16

RMS_EPS = 1e-6
LN_EPS = 1e-5
L2_EPS = 1e-12
LNX_EPS = 64e-5
CONV_WIDTH = 31
CONV_STATE = CONV_WIDTH - 1
CONV_PAD = 32
RWKV_HEAD = 64
XA_HEADS = 4
SCAN_CHUNK = 64
LANES = 128
SEG_TILE = 256
VMEM_LIMIT = 56 * 1024 * 1024

NT = (((1,), (1,)), ((), ()))


def _params(n_axes, sem=None):
    if sem is None:
        sem = ("arbitrary",) * n_axes
    return pltpu.CompilerParams(dimension_semantics=sem, vmem_limit_bytes=VMEM_LIMIT)


def _resident(shape):
    nd = len(shape)
    return pl.BlockSpec(tuple(shape), lambda *_: (0,) * nd, pipeline_mode=pl.Buffered(1))


def _rms(x, g):
    ms = jnp.mean(x * x, axis=-1, keepdims=True)
    return x * lax.rsqrt(ms + RMS_EPS) * g


def _bdot(a, b):
    return jnp.dot(a.astype(BF16), b.astype(BF16), preferred_element_type=F32)


def _split_dot_rhs(x, w, passes):
    acc = None
    rem = x
    for i in range(passes):
        part = rem.astype(BF16)
        term = jnp.dot(part, w, preferred_element_type=F32)
        acc = term if acc is None else acc + term
        if i + 1 < passes:
            rem = rem - part.astype(F32)
    return acc


def _split_dot_lhs(w, x, passes):
    acc = None
    rem = x
    for i in range(passes):
        part = rem.astype(BF16)
        term = jnp.dot(w, part, preferred_element_type=F32)
        acc = term if acc is None else acc + term
        if i + 1 < passes:
            rem = rem - part.astype(F32)
    return acc


def _segsum(x, bd, passes):
    d = x.shape[-1]
    outs = []
    for c in range(d // SEG_TILE):
        outs.append(_split_dot_rhs(x[:, c * SEG_TILE:(c + 1) * SEG_TILE], bd, passes))
    return jnp.concatenate(outs, axis=-1)


def _norm_matmul_kernel(*refs, glu, has_bias):
    if has_bias:
        x_ref, g_ref, w_ref, b_ref, o_ref = refs
    else:
        x_ref, g_ref, w_ref, o_ref = refs
    h = _rms(x_ref[...], g_ref[...]).astype(BF16)
    r = jnp.dot(h, w_ref[...], preferred_element_type=F32)
    if has_bias:
        r = r + b_ref[...]
    if glu:
        d = r.shape[-1] // 2
        r = r[:, :d] * jax.nn.sigmoid(r[:, d:])
    o_ref[...] = r.astype(o_ref.dtype)


def _norm_matmul(x, g, w, b=None, *, glu=False, out_dtype=F32, tm=512):
    m, d = x.shape
    n = w.shape[1]
    n_out = n // 2 if glu else n
    tm = min(tm, m)
    in_specs = [pl.BlockSpec((tm, d), lambda i: (i, 0)), _resident((1, d)), _resident((d, n))]
    args = [x, g.reshape(1, d), w]
    if b is not None:
        in_specs.append(_resident((1, n)))
        args.append(b.reshape(1, n))
    return pl.pallas_call(
        functools.partial(_norm_matmul_kernel, glu=glu, has_bias=b is not None),
        grid=(m // tm,),
        in_specs=in_specs,
        out_specs=pl.BlockSpec((tm, n_out), lambda i: (i, 0)),
        out_shape=jax.ShapeDtypeStruct((m, n_out), out_dtype),
        compiler_params=_params(1),
        name="norm_matmul",
    )(*args)


def _matmul_res_kernel(*refs, has_bias):
    if has_bias:
        a_ref, w_ref, b_ref, x_ref, o_ref = refs
    else:
        a_ref, w_ref, x_ref, o_ref = refs
    r = jnp.dot(a_ref[...].astype(BF16), w_ref[...], preferred_element_type=F32)
    if has_bias:
        r = r + b_ref[...]
    o_ref[...] = x_ref[...] + r


def _matmul_res(a, w, x, b=None, *, tm=512):
    m, k = a.shape
    d = w.shape[1]
    tm = min(tm, m)
    in_specs = [pl.BlockSpec((tm, k), lambda i: (i, 0)), _resident((k, d))]
    args = [a, w]
    if b is not None:
        in_specs.append(_resident((1, d)))
        args.append(b.reshape(1, d))
    in_specs.append(pl.BlockSpec((tm, d), lambda i: (i, 0)))
    args.append(x)
    return pl.pallas_call(
        functools.partial(_matmul_res_kernel, has_bias=b is not None),
        grid=(m // tm,),
        in_specs=in_specs,
        out_specs=pl.BlockSpec((tm, d), lambda i: (i, 0)),
        out_shape=jax.ShapeDtypeStruct((m, d), F32),
        compiler_params=_params(1),
        name="matmul_res",
    )(*args)


def _mlp_kernel(x_ref, g_ref, wu_ref, wd_ref, fg_ref, o_ref, *, final_norm, n_chunks):
    x = x_ref[...]
    h = _rms(x, g_ref[...]).astype(BF16)
    ck = wu_ref.shape[1] // n_chunks
    acc = x
    for c in range(n_chunks):
        u = jnp.dot(h, wu_ref[:, c * ck:(c + 1) * ck], preferred_element_type=F32)
        u = jnp.maximum(u, 0.0)
        u = (u * u).astype(BF16)
        acc = acc + jnp.dot(u, wd_ref[c * ck:(c + 1) * ck, :], preferred_element_type=F32)
    if final_norm:
        acc = _rms(acc, fg_ref[...])
    o_ref[...] = acc


def _mlp(x, g, w_up, w_down, final_g, *, final_norm, tm=512):
    m, d = x.shape
    dff = w_up.shape[1]
    tm = min(tm, m)
    return pl.pallas_call(
        functools.partial(_mlp_kernel, final_norm=final_norm, n_chunks=4),
        grid=(m // tm,),
        in_specs=[pl.BlockSpec((tm, d), lambda i: (i, 0)), _resident((1, d)),
                  _resident((d, dff)), _resident((dff, d)), _resident((1, d))],
        out_specs=pl.BlockSpec((tm, d), lambda i: (i, 0)),
        out_shape=jax.ShapeDtypeStruct((m, d), F32),
        compiler_params=_params(1),
        name="mlp",
    )(x, g.reshape(1, d), w_up, w_down, final_g.reshape(1, d))


def _memkv_kernel(x_ref, w_ref, k_ref, v_ref):
    r = jnp.dot(x_ref[...].astype(BF16), w_ref[0], preferred_element_type=F32)
    d = r.shape[-1] // 2
    k_ref[0] = r[:, :d]
    v_ref[0] = r[:, d:]


def _memkv(mem, w_kv, *, tm=512):
    m, d = mem.shape
    depth = w_kv.shape[0]
    out = jax.ShapeDtypeStruct((depth, m, d), F32)
    return pl.pallas_call(
        _memkv_kernel,
        grid=(depth, m // tm),
        in_specs=[pl.BlockSpec((tm, d), lambda l, i: (i, 0)),
                  pl.BlockSpec((1, d, 2 * d), lambda l, i: (l, 0, 0))],
        out_specs=[pl.BlockSpec((1, tm, d), lambda l, i: (l, i, 0)),
                   pl.BlockSpec((1, tm, d), lambda l, i: (l, i, 0))],
        out_shape=[out, out],
        compiler_params=_params(2),
        name="memkv",
    )(mem, w_kv)


def _attn_kernel(q_ref, k_ref, v_ref, o_ref, *, heads, scale):
    bb = q_ref.shape[0]
    hd = q_ref.shape[2] // heads
    for i in range(bb):
        outs = []
        for h in range(heads):
            sl = slice(h * hd, (h + 1) * hd)
            q = q_ref[i, :, sl].astype(BF16)
            k = k_ref[i, :, sl].astype(BF16)
            v = v_ref[i, :, sl].astype(BF16)
            s = lax.dot_general(q, k, NT, preferred_element_type=F32) * scale
            e = jnp.exp(s - jnp.max(s, axis=-1, keepdims=True))
            p = (e / jnp.sum(e, axis=-1, keepdims=True)).astype(BF16)
            outs.append(jnp.dot(p, v, preferred_element_type=F32))
        o_ref[i] = jnp.concatenate(outs, axis=-1).astype(o_ref.dtype)


def _attn(q, mem_k, mem_v, *, bb, tt, out_dtype):
    b, t, d = q.shape
    m = mem_k.shape[1]
    hd = d // XA_HEADS
    return pl.pallas_call(
        functools.partial(_attn_kernel, heads=XA_HEADS, scale=hd ** -0.5),
        grid=(b // bb, t // tt),
        in_specs=[pl.BlockSpec((bb, tt, d), lambda i, j: (i, j, 0)),
                  pl.BlockSpec((bb, m, d), lambda i, j: (i, 0, 0)),
                  pl.BlockSpec((bb, m, d), lambda i, j: (i, 0, 0))],
        out_specs=pl.BlockSpec((bb, tt, d), lambda i, j: (i, j, 0)),
        out_shape=jax.ShapeDtypeStruct((b, t, d), out_dtype),
        compiler_params=_params(2),
        name="attn_core",
    )(q, mem_k, mem_v)


def _conv_kernel(u_ref, c0_ref, f_ref, bdw_ref, lg_ref, lb_ref, z_ref, cs_ref, up_ref, *, rows):
    bb, tt, d = u_ref.shape
    t = pl.program_id(1)
    lo = CONV_PAD - CONV_STATE

    @pl.when(t == 0)
    def _():
        up_ref[:, lo:CONV_PAD, :] = c0_ref[...]

    up_ref[:, CONV_PAD:CONV_PAD + tt, :] = u_ref[...]
    for i in range(bb):
        for r0 in range(0, tt, rows):
            acc = jnp.zeros((rows, d), F32)
            for j in range(CONV_WIDTH):
                acc = acc + up_ref[i, lo + r0 + j:lo + r0 + j + rows, :] * f_ref[j:j + 1, :]
            z = acc + bdw_ref[...]
            zc = z - jnp.mean(z, axis=-1, keepdims=True)
            zn = zc * lax.rsqrt(jnp.mean(zc * zc, axis=-1, keepdims=True) + LN_EPS)
            zn = zn * lg_ref[...] + lb_ref[...]
            z_ref[i, r0:r0 + rows, :] = (zn * jax.nn.sigmoid(zn)).astype(z_ref.dtype)
    tail = up_ref[:, tt + lo:tt + CONV_PAD, :]
    cs_ref[...] = tail
    up_ref[:, lo:CONV_PAD, :] = tail


def _conv_core(u, conv0, filt, b_dw, ln_g, ln_b, *, bb, tt):
    b, t, d = u.shape
    vec = lambda a: a.reshape(1, d)
    return pl.pallas_call(
        functools.partial(_conv_kernel, rows=8),
        grid=(b // bb, t // tt),
        in_specs=[pl.BlockSpec((bb, tt, d), lambda i, j: (i, j, 0)),
                  pl.BlockSpec((bb, CONV_STATE, d), lambda i, j: (i, 0, 0)),
                  _resident((CONV_WIDTH, d)), _resident((1, d)), _resident((1, d)), _resident((1, d))],
        out_specs=[pl.BlockSpec((bb, tt, d), lambda i, j: (i, j, 0)),
                   pl.BlockSpec((bb, CONV_STATE, d), lambda i, j: (i, 0, 0))],
        out_shape=[jax.ShapeDtypeStruct((b, t, d), F32),
                   jax.ShapeDtypeStruct((b, CONV_STATE, d), F32)],
        scratch_shapes=[pltpu.VMEM((bb, CONV_PAD + tt, d), F32)],
        compiler_params=_params(2),
        name="conv_core",
    )(u, conv0, filt, vec(b_dw), vec(ln_g), vec(ln_b))


def _rwkv_proj_kernel(*refs, has_vfirst):
    (x_ref, sh0_ref, g_ref, mix_ref, wr_ref, wk_ref, wv_ref, w0_ref, w1_ref, w2_ref,
     a0_ref, a1_ref, a2_ref, g1_ref, g2_ref, kk_ref, ka_ref, bd_ref) = refs[:18]
    pos = 18
    if has_vfirst:
        vf_ref, v0_ref, v1_ref, v2_ref = refs[pos:pos + 4]
        pos += 4
    (r_o, lw_o, k_o, v_o, kkn_o, b_o, g_o, sh_o) = refs[pos:pos + 8]
    hs_ref = refs[pos + 8]

    bb, tt, d = x_ref.shape
    t = pl.program_id(1)
    h3 = _rms(x_ref[...], g_ref[...])

    @pl.when(t == 0)
    def _():
        hs_ref[:, 7:8, :] = sh0_ref[...]

    hs_ref[:, 8:8 + tt, :] = h3
    prev3 = hs_ref[:, 7:7 + tt, :]
    last = h3[:, tt - 1:tt, :]
    hs_ref[:, 7:8, :] = last
    sh_o[...] = last

    m = bb * tt
    h = h3.reshape(m, d)
    xx = prev3.reshape(m, d) - h
    mix = mix_ref[...]
    xr, xw, xk, xv, xa, xg = ((h + xx * mix[i:i + 1, :]).astype(BF16) for i in range(6))

    r = jnp.dot(xr, wr_ref[...], preferred_element_type=F32)
    k = jnp.dot(xk, wk_ref[...], preferred_element_type=F32)
    v = jnp.dot(xv, wv_ref[...], preferred_element_type=F32)
    w_pre = w0_ref[...] + _bdot(jnp.tanh(jnp.dot(xw, w1_ref[...], preferred_element_type=F32)), w2_ref[...])
    lw = -math.exp(-0.5) * jax.nn.sigmoid(w_pre)
    if has_vfirst:
        gate = jax.nn.sigmoid(v0_ref[...] + _bdot(jnp.dot(xv, v1_ref[...], preferred_element_type=F32), v2_ref[...]))
        v = v + (vf_ref[...].reshape(m, d) - v) * gate
    a = jax.nn.sigmoid(a0_ref[...] + _bdot(jnp.dot(xa, a1_ref[...], preferred_element_type=F32), a2_ref[...]))
    g = _bdot(jax.nn.sigmoid(jnp.dot(xg, g1_ref[...], preferred_element_type=F32)), g2_ref[...])
    kk = k * kk_ref[...]
    nrm = jnp.maximum(jnp.sqrt(_segsum(kk * kk, bd_ref[...], 3)), L2_EPS)
    kk = kk / nrm
    k = k * (1.0 + (a - 1.0) * ka_ref[...])

    shp = (bb, tt, d)
    r_o[...] = r.reshape(shp)
    lw_o[...] = lw.reshape(shp)
    k_o[...] = k.reshape(shp)
    v_o[...] = v.reshape(shp)
    kkn_o[...] = kk.reshape(shp)
    b_o[...] = (kk * a).reshape(shp)
    g_o[...] = g.reshape(shp)


def _rwkv_proj(x, shift0, g, p, ri, v_first, bd, *, bb, tt):
    b, t, d = x.shape
    vec = lambda a: a.reshape(1, d)
    tile = pl.BlockSpec((bb, tt, d), lambda i, j: (i, j, 0))
    row = pl.BlockSpec((bb, 1, d), lambda i, j: (i, 0, 0))
    args = [x, shift0.reshape(b, 1, d), vec(g), p['rw_mix'][ri],
            p['rw_w_r'][ri], p['rw_w_k'][ri], p['rw_w_v'][ri],
            vec(p['rw_w0'][ri]), p['rw_w1'][ri], p['rw_w2'][ri],
            vec(p['rw_a0'][ri]), p['rw_a1'][ri], p['rw_a2'][ri],
            p['rw_g1'][ri], p['rw_g2'][ri], vec(p['rw_k_k'][ri]), vec(p['rw_k_a'][ri]), bd]
    in_specs = [tile, row] + [_resident(a.shape) for a in args[2:]]
    has_vfirst = v_first is not None
    if has_vfirst:
        vi = ri - 1
        extra = [v_first, vec(p['rw_v0'][vi]), p['rw_v1'][vi], p['rw_v2'][vi]]
        args += extra
        in_specs += [tile] + [_resident(a.shape) for a in extra[1:]]
    big = jax.ShapeDtypeStruct((b, t, d), F32)
    outs = pl.pallas_call(
        functools.partial(_rwkv_proj_kernel, has_vfirst=has_vfirst),
        grid=(b // bb, t // tt),
        in_specs=in_specs,
        out_specs=[tile] * 7 + [row],
        out_shape=[big] * 7 + [jax.ShapeDtypeStruct((b, 1, d), F32)],
        scratch_shapes=[pltpu.VMEM((bb, tt + 8, d), F32)],
        compiler_params=_params(2),
        name="rwkv_proj",
    )(*args)
    return outs


def _scan_kernel(r_ref, lw_ref, k_ref, v_ref, kk_ref, b_ref, s0_ref, tri_ref, y_ref, so_ref, s_scr, *, tc):
    C = SCAN_CHUNK
    d = r_ref.shape[2]
    n = RWKV_HEAD
    t = pl.program_id(1)

    @pl.when(t == 0)
    def _():
        s_scr[...] = s0_ref[0]

    def load(ref):
        x = ref[0]
        if tc < C:
            x = jnp.concatenate([x, jnp.zeros((C - tc, d), F32)], axis=0)
        return x

    r, lw, k, v, kk, b = (load(ref) for ref in (r_ref, lw_ref, k_ref, v_ref, kk_ref, b_ref))
    cl = _split_dot_lhs(tri_ref[...], lw, 3)
    cl_end = cl[C - 1:C, :]
    e_neg = jnp.exp(-cl)
    e_end = jnp.exp(cl_end - cl)
    a_hat = -kk * jnp.exp(cl - lw)
    r_hat = r * jnp.exp(cl)
    b_hat = b * e_neg
    k_hat = k * e_neg
    b_bar = b * e_end
    k_bar = k * e_end
    w_end = jnp.exp(cl_end)

    lane_lo = lax.broadcasted_iota(jnp.int32, (1, LANES), 1) < n
    row = lax.broadcasted_iota(jnp.int32, (2 * C, 2 * C), 0)
    col = lax.broadcasted_iota(jnp.int32, (2 * C, 2 * C), 1)
    row_hi = row >= C
    col_hi = col >= C
    tr = jnp.where(row_hi, row - C, row)
    tj = jnp.where(col_hi, col - C, col)
    diag = row_hi == col_hi
    m_l = jnp.logical_and(diag, tj < tr)
    m_ak = jnp.logical_and(jnp.logical_not(diag), tj < tr)
    m_incl = tj <= tr
    eye = (row == col).astype(F32)
    rl_row = lax.broadcasted_iota(jnp.int32, (2 * C, LANES), 0) >= C
    rl_lane = lax.broadcasted_iota(jnp.int32, (2 * C, LANES), 1) >= n
    m_rl = rl_row == rl_lane

    def stack_masked(x):
        return jnp.concatenate([jnp.where(lane_lo, x, 0.0), jnp.where(lane_lo, 0.0, x)], axis=0).astype(BF16)

    n_pairs = d // LANES
    for p in range(n_pairs):
        sl = slice(p * LANES, (p + 1) * LANES)
        vv = v[:, sl]
        bh, kh = b_hat[:, sl], k_hat[:, sl]
        s_old = s_scr[p]
        lhs_a = stack_masked(a_hat[:, sl])
        lhs_r = stack_masked(r_hat[:, sl])
        rhs = jnp.concatenate([jnp.where(lane_lo, bh, kh), jnp.where(lane_lo, kh, bh), s_old, s_old],
                              axis=0).astype(BF16)
        gp_a = lax.dot_general(lhs_a, rhs, NT, preferred_element_type=F32)
        gp_r = lax.dot_general(lhs_r, rhs, NT, preferred_element_type=F32)
        g_a, p_a = gp_a[:, :2 * C], gp_a[:, 2 * C:]
        g_r, p_r = gp_r[:, :2 * C], gp_r[:, 2 * C:]

        l_mat = jnp.where(m_l, g_a, 0.0)
        ak = jnp.where(m_ak, g_a, 0.0)
        vv2 = jnp.concatenate([vv, vv], axis=0)
        rhs_u = jnp.where(m_rl, p_a + _bdot(ak, vv2), 0.0)

        t_mat = eye + l_mat
        l_pow = l_mat
        for _ in range(int(math.log2(C)) - 1):
            l_pow = _bdot(l_pow, l_pow)
            t_mat = t_mat + _bdot(t_mat, l_pow)

        u_st = _bdot(t_mat, rhs_u)
        u_lo, u_hi = u_st[:C], u_st[C:]
        w_mix = jnp.concatenate([jnp.where(lane_lo, u_lo, vv), jnp.where(lane_lo, vv, u_hi)], axis=0)
        y_st = jnp.where(m_rl, p_r + _bdot(jnp.where(m_incl, g_r, 0.0), w_mix), 0.0)
        y = y_st[:C] + y_st[C:]
        y_ref[0, :, sl] = y[:tc]

        uv_t = jnp.concatenate([u_lo + u_hi, vv], axis=0).T
        bk = jnp.concatenate([b_bar[:, sl], k_bar[:, sl]], axis=0)
        full = _bdot(uv_t, bk)
        s_new = s_old * w_end[:, sl] + jnp.where(lane_lo, full[:n], full[n:])
        s_scr[p] = s_new
        so_ref[0, p] = s_new


def _rwkv_scan(r, lw, k, v, kk, b, s0_pair, tri, *, tc):
    bsz, t, d = r.shape
    n_pairs = d // LANES
    tile = pl.BlockSpec((1, tc, d), lambda i, j: (i, j, 0))
    st = pl.BlockSpec((1, n_pairs, RWKV_HEAD, LANES), lambda i, j: (i, 0, 0, 0))
    return pl.pallas_call(
        functools.partial(_scan_kernel, tc=tc),
        grid=(bsz, t // tc),
        in_specs=[tile] * 6 + [st, _resident(tri.shape)],
        out_specs=[tile, st],
        out_shape=[jax.ShapeDtypeStruct((bsz, t, d), F32),
                   jax.ShapeDtypeStruct((bsz, n_pairs, RWKV_HEAD, LANES), F32)],
        scratch_shapes=[pltpu.VMEM((n_pairs, RWKV_HEAD, LANES), F32)],
        compiler_params=_params(2),
        name="rwkv_scan",
    )(r, lw, k, v, kk, b, s0_pair, tri)


def _rwkv_out_kernel(y_ref, r_ref, k_ref, v_ref, g_ref, x_ref, lg_ref, lb_ref, rk_ref, wo_ref, bd_ref, o_ref):
    bd = bd_ref[...]
    inv_n = 1.0 / RWKV_HEAD
    y = y_ref[...]
    yc = y - _segsum(y, bd, 2) * inv_n
    var = _segsum(yc * yc, bd, 2) * inv_n
    yn = yc * lax.rsqrt(var + LNX_EPS) * lg_ref[...] + lb_ref[...]
    v = v_ref[...]
    bonus = _segsum(r_ref[...] * k_ref[...] * rk_ref[...], bd, 2) * v
    z = ((yn + bonus) * g_ref[...]).astype(BF16)
    o_ref[...] = x_ref[...] + jnp.dot(z, wo_ref[...], preferred_element_type=F32)


def _rwkv_out(y, r, k, v, g, x, lnx_g, lnx_b, r_k, w_o, bd, *, tm=256):
    m, d = x.shape
    tm = min(tm, m)
    tile = pl.BlockSpec((tm, d), lambda i: (i, 0))
    vec = lambda a: a.reshape(1, d)
    return pl.pallas_call(
        _rwkv_out_kernel,
        grid=(m // tm,),
        in_specs=[tile] * 6 + [_resident((1, d))] * 3 + [_resident(w_o.shape), _resident(bd.shape)],
        out_specs=tile,
        out_shape=jax.ShapeDtypeStruct((m, d), F32),
        compiler_params=_params(1),
        name="rwkv_out",
    )(y, r, k, v, g, x, vec(lnx_g), vec(lnx_b), vec(r_k), w_o, bd)


def _to_pair(s):
    b, h, n, _ = s.shape
    return s.reshape(b, h // 2, 2, n, n).transpose(0, 1, 3, 2, 4).reshape(b, h // 2, n, 2 * n)


def _from_pair(sp):
    b, hp, n, _ = sp.shape
    return sp.reshape(b, hp, n, 2, n).transpose(0, 1, 3, 2, 4).reshape(b, 2 * hp, n, n)


def _trunk(x, mem_k, mem_v, conv_state, shift_state, wkv_state, p, consts, cfg):
    bsz, t, d = x.shape
    m = bsz * t
    depth = p['norm_g'].shape[0]
    bd, tri = consts
    x2 = x.reshape(m, d)
    conv_new, shift_new, wkv_new = [], [], []
    v_first = None
    for layer in range(depth):
        g = p['norm_g'][layer]
        if layer % 2 == 0:
            ci = layer // 2
            u = _norm_matmul(x2, g[0], p['cv_w_in'][ci], p['cv_b_in'][ci], glu=True)
            z, cs = _conv_core(u.reshape(bsz, t, d), conv_state[ci], p['cv_w_dw'][ci], p['cv_b_dw'][ci],
                               p['cv_ln_g'][ci], p['cv_ln_b'][ci], bb=cfg['conv_bb'], tt=cfg['conv_tt'])
            conv_new.append(cs)
            x2 = _matmul_res(z.reshape(m, d), p['cv_w_out'][ci], x2, p['cv_b_out'][ci])
        else:
            ri = layer // 2
            r, lw, k, v, kk, b, gt, sh = _rwkv_proj(x2.reshape(bsz, t, d), shift_state[ri], g[0], p, ri, v_first, bd,
                                                    bb=cfg['proj_bb'], tt=cfg['proj_tt'])
            if v_first is None:
                v_first = v
            y, s_new = _rwkv_scan(r, lw, k, v, kk, b, _to_pair(wkv_state[ri]), tri, tc=cfg['scan_tc'])
            shift_new.append(sh.reshape(bsz, d))
            wkv_new.append(_from_pair(s_new))
            flat = lambda a: a.reshape(m, d)
            x2 = _rwkv_out(flat(y), flat(r), flat(k), flat(v), flat(gt), x2, p['rw_lnx_g'][ri], p['rw_lnx_b'][ri],
                           p['rw_r_k'][ri].reshape(d), p['rw_w_o'][ri], bd)
        q = _norm_matmul(x2, g[1], p['xa_w_q'][layer], out_dtype=cfg['attn_dtype'])
        mk = mem_k[layer].reshape(bsz, -1, d)
        mv = mem_v[layer].reshape(bsz, -1, d)
        o = _attn(q.reshape(bsz, t, d), mk, mv, bb=cfg['attn_bb'], tt=cfg['attn_tt'], out_dtype=cfg['attn_dtype'])
        x2 = _matmul_res(o.reshape(m, d), p['xa_w_o'][layer], x2)
        x2 = _mlp(x2, g[2], p['mlp_w_up'][layer], p['mlp_w_down'][layer], p['final_g'],
                  final_norm=layer == depth - 1)
    return x2.reshape(bsz, t, d), jnp.stack(conv_new), jnp.stack(shift_new), jnp.stack(wkv_new)


PROMPT_CFG = dict(conv_bb=1, conv_tt=128, proj_bb=1, proj_tt=256, scan_tc=SCAN_CHUNK,
                  attn_bb=1, attn_tt=512, attn_dtype=BF16)
SAMPLE_CFG = dict(conv_bb=16, conv_tt=8, proj_bb=16, proj_tt=8, scan_tc=8,
                  attn_bb=4, attn_tt=8, attn_dtype=F32)


def kernel(x_prompt, x_sample, mem_prompt, cache_mem_k, cache_mem_v, state_conv, state_shift, state_wkv, norm_g, final_g, cv_w_in, cv_b_in, cv_w_dw, cv_b_dw, cv_ln_g, cv_ln_b, cv_w_out, cv_b_out, rw_mix, rw_w_r, rw_w_k, rw_w_v, rw_w_o, rw_w0, rw_w1, rw_w2, rw_a0, rw_a1, rw_a2, rw_v0, rw_v1, rw_v2, rw_g1, rw_g2, rw_k_k, rw_k_a, rw_r_k, rw_lnx_g, rw_lnx_b, xa_w_q, xa_w_kv, xa_w_o, mlp_w_up, mlp_w_down):
    bf = lambda w: w.astype(BF16)
    p = {
        'norm_g': norm_g, 'final_g': final_g,
        'cv_w_in': bf(cv_w_in), 'cv_b_in': cv_b_in, 'cv_w_dw': cv_w_dw, 'cv_b_dw': cv_b_dw,
        'cv_ln_g': cv_ln_g, 'cv_ln_b': cv_ln_b, 'cv_w_out': bf(cv_w_out), 'cv_b_out': cv_b_out,
        'rw_mix': rw_mix, 'rw_w_r': bf(rw_w_r), 'rw_w_k': bf(rw_w_k), 'rw_w_v': bf(rw_w_v), 'rw_w_o': bf(rw_w_o),
        'rw_w0': rw_w0, 'rw_w1': bf(rw_w1), 'rw_w2': bf(rw_w2), 'rw_a0': rw_a0, 'rw_a1': bf(rw_a1),
        'rw_a2': bf(rw_a2), 'rw_v0': rw_v0, 'rw_v1': bf(rw_v1), 'rw_v2': bf(rw_v2), 'rw_g1': bf(rw_g1),
        'rw_g2': bf(rw_g2), 'rw_k_k': rw_k_k, 'rw_k_a': rw_k_a, 'rw_r_k': rw_r_k,
        'rw_lnx_g': rw_lnx_g, 'rw_lnx_b': rw_lnx_b,
        'xa_w_q': bf(xa_w_q), 'xa_w_o': bf(xa_w_o), 'mlp_w_up': bf(mlp_w_up), 'mlp_w_down': bf(mlp_w_down),
    }
    depth = norm_g.shape[0]
    bp, mlen, d = mem_prompt.shape
    n_conv, n_rwkv = state_conv.shape[0], state_wkv.shape[0]
    heads, n = state_wkv.shape[2], state_wkv.shape[3]
    hd = d // XA_HEADS

    idx = jnp.arange(SEG_TILE) // RWKV_HEAD
    bd = (idx[:, None] == idx[None, :]).astype(BF16)
    ar = jnp.arange(SCAN_CHUNK)
    tri = (ar[None, :] <= ar[:, None]).astype(BF16)
    consts = (bd, tri)

    mk, mv = _memkv(mem_prompt.reshape(bp * mlen, d), bf(xa_w_kv))
    mem_k_prompt = mk.reshape(depth, bp, mlen, XA_HEADS, hd)
    mem_v_prompt = mv.reshape(depth, bp, mlen, XA_HEADS, hd)

    conv0 = jnp.zeros((n_conv, bp, CONV_STATE, d), F32)
    shift0 = jnp.zeros((n_rwkv, bp, d), F32)
    wkv0 = jnp.zeros((n_rwkv, bp, heads, n, n), F32)
    y_prompt, conv_prompt, shift_prompt, wkv_prompt = _trunk(
        x_prompt, mem_k_prompt, mem_v_prompt, conv0, shift0, wkv0, p, consts, PROMPT_CFG)
    y_sample, conv_sample, shift_sample, wkv_sample = _trunk(
        x_sample, cache_mem_k, cache_mem_v, state_conv, state_shift, state_wkv, p, consts, SAMPLE_CFG)
    return (y_prompt, y_sample, mem_k_prompt, mem_v_prompt, conv_prompt, shift_prompt, wkv_prompt,
            conv_sample, shift_sample, wkv_sample)
```

```python
import functools
import math

import jax
import jax.numpy as jnp
from jax import lax
from jax.experimental import pallas as pl
from jax.experimental.pallas import tpu as pltpu

F32 = jnp.float32
BF16 = jnp.bfloat16

RMS_EPS = 1e-6
LN_EPS = 1e-5
L2_EPS = 1e-12
LNX_EPS = 64e-5
CONV_WIDTH = 31
CONV_STATE = CONV_WIDTH - 1
CONV_PAD = 32
RWKV_HEAD = 64
XA_HEADS = 4
SCAN_CHUNK = 64
LANES = 128
SUBLANES = 8
SEG_TILE = 256
VMEM_LIMIT = 56 * 1024 * 1024

NT = (((1,), (1,)), ((), ()))


def _params(n_axes, sem=None):
    if sem is None:
        sem = ("arbitrary",) * n_axes
    return pltpu.CompilerParams(dimension_semantics=sem, vmem_limit_bytes=VMEM_LIMIT)


def _resident(shape):
    nd = len(shape)
    return pl.BlockSpec(tuple(shape), lambda *_: (0,) * nd, pipeline_mode=pl.Buffered(1))


def _rms(x, g):
    ms = jnp.mean(x * x, axis=-1, keepdims=True)
    return x * lax.rsqrt(ms + RMS_EPS) * g


def _bdot(a, b):
    return jnp.dot(a.astype(BF16), b.astype(BF16), preferred_element_type=F32)


def _split_dot_rhs(x, w, passes):
    acc = None
    rem = x
    for i in range(passes):
        part = rem.astype(BF16)
        term = jnp.dot(part, w, preferred_element_type=F32)
        acc = term if acc is None else acc + term
        if i + 1 < passes:
            rem = rem - part.astype(F32)
    return acc


def _split_dot_lhs(w, x, passes):
    acc = None
    rem = x
    for i in range(passes):
        part = rem.astype(BF16)
        term = jnp.dot(w, part, preferred_element_type=F32)
        acc = term if acc is None else acc + term
        if i + 1 < passes:
            rem = rem - part.astype(F32)
    return acc


def _segsum(x, bd, passes):
    d = x.shape[-1]
    outs = []
    for c in range(d // SEG_TILE):
        outs.append(_split_dot_rhs(x[:, c * SEG_TILE:(c + 1) * SEG_TILE], bd, passes))
    return jnp.concatenate(outs, axis=-1)


def _norm_matmul_kernel(*refs, glu, has_bias):
    if has_bias:
        x_ref, g_ref, w_ref, b_ref, o_ref = refs
    else:
        x_ref, g_ref, w_ref, o_ref = refs
    h = _rms(x_ref[...], g_ref[...]).astype(BF16)
    r = jnp.dot(h, w_ref[...], preferred_element_type=F32)
    if has_bias:
        r = r + b_ref[...]
    if glu:
        d = r.shape[-1] // 2
        r = r[:, :d] * jax.nn.sigmoid(r[:, d:])
    o_ref[...] = r.astype(o_ref.dtype)


def _norm_matmul(x, g, w, b=None, *, glu=False, out_dtype=F32, tm=512):
    m, d = x.shape
    n = w.shape[1]
    n_out = n // 2 if glu else n
    tm = min(tm, m)
    in_specs = [pl.BlockSpec((tm, d), lambda i: (i, 0)), _resident((1, d)), _resident((d, n))]
    args = [x, g.reshape(1, d), w]
    if b is not None:
        in_specs.append(_resident((1, n)))
        args.append(b.reshape(1, n))
    return pl.pallas_call(
        functools.partial(_norm_matmul_kernel, glu=glu, has_bias=b is not None),
        grid=(m // tm,),
        in_specs=in_specs,
        out_specs=pl.BlockSpec((tm, n_out), lambda i: (i, 0)),
        out_shape=jax.ShapeDtypeStruct((m, n_out), out_dtype),
        compiler_params=_params(1),
        name="norm_matmul",
    )(*args)


def _matmul_res_kernel(*refs, has_bias):
    if has_bias:
        a_ref, w_ref, b_ref, x_ref, o_ref = refs
    else:
        a_ref, w_ref, x_ref, o_ref = refs
    r = jnp.dot(a_ref[...].astype(BF16), w_ref[...], preferred_element_type=F32)
    if has_bias:
        r = r + b_ref[...]
    o_ref[...] = x_ref[...] + r


def _matmul_res(a, w, x, b=None, *, tm=512):
    m, k = a.shape
    d = w.shape[1]
    tm = min(tm, m)
    in_specs = [pl.BlockSpec((tm, k), lambda i: (i, 0)), _resident((k, d))]
    args = [a, w]
    if b is not None:
        in_specs.append(_resident((1, d)))
        args.append(b.reshape(1, d))
    in_specs.append(pl.BlockSpec((tm, d), lambda i: (i, 0)))
    args.append(x)
    return pl.pallas_call(
        functools.partial(_matmul_res_kernel, has_bias=b is not None),
        grid=(m // tm,),
        in_specs=in_specs,
        out_specs=pl.BlockSpec((tm, d), lambda i: (i, 0)),
        out_shape=jax.ShapeDtypeStruct((m, d), F32),
        compiler_params=_params(1),
        name="matmul_res",
    )(*args)


def _mlp_kernel(x_ref, g_ref, wu_ref, wd_ref, fg_ref, o_ref, *, final_norm, n_chunks):
    x = x_ref[...]
    h = _rms(x, g_ref[...]).astype(BF16)
    ck = wu_ref.shape[1] // n_chunks
    acc = x
    for c in range(n_chunks):
        u = jnp.dot(h, wu_ref[:, c * ck:(c + 1) * ck], preferred_element_type=F32)
        u = jnp.maximum(u, 0.0)
        u = (u * u).astype(BF16)
        acc = acc + jnp.dot(u, wd_ref[c * ck:(c + 1) * ck, :], preferred_element_type=F32)
    if final_norm:
        acc = _rms(acc, fg_ref[...])
    o_ref[...] = acc


def _mlp(x, g, w_up, w_down, final_g, *, final_norm, tm=512):
    m, d = x.shape
    dff = w_up.shape[1]
    tm = min(tm, m)
    return pl.pallas_call(
        functools.partial(_mlp_kernel, final_norm=final_norm, n_chunks=4),
        grid=(m // tm,),
        in_specs=[pl.BlockSpec((tm, d), lambda i: (i, 0)), _resident((1, d)),
                  _resident((d, dff)), _resident((dff, d)), _resident((1, d))],
        out_specs=pl.BlockSpec((tm, d), lambda i: (i, 0)),
        out_shape=jax.ShapeDtypeStruct((m, d), F32),
        compiler_params=_params(1),
        name="mlp",
    )(x, g.reshape(1, d), w_up, w_down, final_g.reshape(1, d))


def _memkv_kernel(x_ref, w_ref, k_ref, v_ref):
    r = jnp.dot(x_ref[...].astype(BF16), w_ref[0], preferred_element_type=F32)
    tb, mlen, heads, hd = k_ref.shape
    d = heads * hd
    for s in range(tb):
        rows = slice(s * mlen, (s + 1) * mlen)
        for h in range(heads):
            k_ref[s, :, h, :] = r[rows, h * hd:(h + 1) * hd]
            v_ref[s, :, h, :] = r[rows, d + h * hd:d + (h + 1) * hd]


def _memkv(mem, w_kv, *, tb=2):
    bp, mlen, d = mem.shape
    depth = w_kv.shape[0]
    hd = d // XA_HEADS
    out = jax.ShapeDtypeStruct((depth, bp, mlen, XA_HEADS, hd), F32)
    blk = pl.BlockSpec((None, tb, mlen, XA_HEADS, hd), lambda l, i: (l, i, 0, 0, 0))
    return pl.pallas_call(
        _memkv_kernel,
        grid=(depth, bp // tb),
        in_specs=[pl.BlockSpec((tb * mlen, d), lambda l, i: (i, 0)),
                  pl.BlockSpec((1, d, 2 * d), lambda l, i: (l, 0, 0))],
        out_specs=[blk, blk],
        out_shape=[out, out],
        compiler_params=_params(2),
        name="memkv",
    )(mem.reshape(bp * mlen, d), w_kv)


def _attn_kernel(q_ref, k_ref, v_ref, o_ref, *, scale):
    bb = q_ref.shape[0]
    heads, hd = k_ref.shape[2], k_ref.shape[3]
    for i in range(bb):
        outs = []
        for h in range(heads):
            q = q_ref[i, :, h * hd:(h + 1) * hd].astype(BF16)
            k = k_ref[i, :, h, :].astype(BF16)
            v = v_ref[i, :, h, :].astype(BF16)
            s = lax.dot_general(q, k, NT, preferred_element_type=F32) * scale
            e = jnp.exp(s - jnp.max(s, axis=-1, keepdims=True))
            p = (e / jnp.sum(e, axis=-1, keepdims=True)).astype(BF16)
            outs.append(jnp.dot(p, v, preferred_element_type=F32))
        o_ref[i] = jnp.concatenate(outs, axis=-1).astype(o_ref.dtype)


def _attn(q, mem_k, mem_v, layer, *, bb, tt, out_dtype):
    b, t, d = q.shape
    _, _, m, heads, hd = mem_k.shape
    kv = pl.BlockSpec((None, bb, m, heads, hd), lambda i, j: (layer, i, 0, 0, 0))
    return pl.pallas_call(
        functools.partial(_attn_kernel, scale=hd ** -0.5),
        grid=(b // bb, t // tt),
        in_specs=[pl.BlockSpec((bb, tt, d), lambda i, j: (i, j, 0)), kv, kv],
        out_specs=pl.BlockSpec((bb, tt, d), lambda i, j: (i, j, 0)),
        out_shape=jax.ShapeDtypeStruct((b, t, d), out_dtype),
        compiler_params=_params(2),
        name="attn_core",
    )(q, mem_k, mem_v)


def _conv_kernel(u_ref, c0_ref, f_ref, bdw_ref, lg_ref, lb_ref, z_ref, cs_ref, up_ref, *, lane_chunk):
    bb, tt, d = u_ref.shape
    t = pl.program_id(1)
    lo = CONV_PAD - CONV_STATE

    @pl.when(t == 0)
    def _():
        up_ref[:, 0:lo, :] = jnp.zeros((bb, lo, d), F32)
        up_ref[:, CONV_PAD + tt:, :] = jnp.zeros((bb, SUBLANES, d), F32)
        up_ref[:, lo:CONV_PAD, :] = c0_ref[...]

    up_ref[:, CONV_PAD:CONV_PAD + tt, :] = u_ref[...]
    for i in range(bb):
        zs = []
        for c in range(d // lane_chunk):
            cs = slice(c * lane_chunk, (c + 1) * lane_chunk)
            acc = None
            for o in range(SUBLANES):
                g = None
                for j in range(CONV_WIDTH):
                    if (lo + j) % SUBLANES != o:
                        continue
                    base = lo + j - o
                    term = up_ref[i, base:base + tt + SUBLANES, cs] * f_ref[j:j + 1, cs]
                    g = term if g is None else g + term
                shifted = g[o:o + tt]
                acc = shifted if acc is None else acc + shifted
            zs.append(acc)
        z = jnp.concatenate(zs, axis=-1) + bdw_ref[...]
        zc = z - jnp.mean(z, axis=-1, keepdims=True)
        zn = zc * lax.rsqrt(jnp.mean(zc * zc, axis=-1, keepdims=True) + LN_EPS)
        zn = zn * lg_ref[...] + lb_ref[...]
        z_ref[i] = (zn * jax.nn.sigmoid(zn)).astype(z_ref.dtype)
    tail = up_ref[:, tt + lo:tt + CONV_PAD, :]
    cs_ref[...] = tail
    up_ref[:, lo:CONV_PAD, :] = tail


def _conv_core(u, conv0, filt, b_dw, ln_g, ln_b, *, bb, tt):
    b, t, d = u.shape
    vec = lambda a: a.reshape(1, d)
    return pl.pallas_call(
        functools.partial(_conv_kernel, lane_chunk=2 * LANES),
        grid=(b // bb, t // tt),
        in_specs=[pl.BlockSpec((bb, tt, d), lambda i, j: (i, j, 0)),
                  pl.BlockSpec((bb, CONV_STATE, d), lambda i, j: (i, 0, 0)),
                  _resident((CONV_WIDTH, d)), _resident((1, d)), _resident((1, d)), _resident((1, d))],
        out_specs=[pl.BlockSpec((bb, tt, d), lambda i, j: (i, j, 0)),
                   pl.BlockSpec((bb, CONV_STATE, d), lambda i, j: (i, 0, 0))],
        out_shape=[jax.ShapeDtypeStruct((b, t, d), F32),
                   jax.ShapeDtypeStruct((b, CONV_STATE, d), F32)],
        scratch_shapes=[pltpu.VMEM((bb, CONV_PAD + tt + SUBLANES, d), F32)],
        compiler_params=_params(2),
        name="conv_core",
    )(u, conv0, filt, vec(b_dw), vec(ln_g), vec(ln_b))


def _rwkv_proj_kernel(*refs, has_vfirst):
    (x_ref, sh0_ref, g_ref, mix_ref, wr_ref, wk_ref, wv_ref, w0_ref, w1_ref, w2_ref,
     a0_ref, a1_ref, a2_ref, g1_ref, g2_ref, kk_ref, ka_ref, bd_ref) = refs[:18]
    pos = 18
    if has_vfirst:
        vf_ref, v0_ref, v1_ref, v2_ref = refs[pos:pos + 4]
        pos += 4
    (r_o, lw_o, k_o, v_o, kkn_o, b_o, g_o, sh_o) = refs[pos:pos + 8]
    hs_ref = refs[pos + 8]

    bb, tt, d = x_ref.shape
    t = pl.program_id(1)
    h3 = _rms(x_ref[...], g_ref[...])

    @pl.when(t == 0)
    def _():
        hs_ref[:, 7:8, :] = sh0_ref[...]

    hs_ref[:, 8:8 + tt, :] = h3
    prev3 = hs_ref[:, 7:7 + tt, :]
    last = h3[:, tt - 1:tt, :]
    hs_ref[:, 7:8, :] = last
    sh_o[...] = last

    m = bb * tt
    h = h3.reshape(m, d)
    xx = prev3.reshape(m, d) - h
    mix = mix_ref[...]
    xr, xw, xk, xv, xa, xg = ((h + xx * mix[i:i + 1, :]).astype(BF16) for i in range(6))

    r = jnp.dot(xr, wr_ref[...], preferred_element_type=F32)
    k = jnp.dot(xk, wk_ref[...], preferred_element_type=F32)
    v = jnp.dot(xv, wv_ref[...], preferred_element_type=F32)
    w_pre = w0_ref[...] + _bdot(jnp.tanh(jnp.dot(xw, w1_ref[...], preferred_element_type=F32)), w2_ref[...])
    lw = -math.exp(-0.5) * jax.nn.sigmoid(w_pre)
    if has_vfirst:
        gate = jax.nn.sigmoid(v0_ref[...] + _bdot(jnp.dot(xv, v1_ref[...], preferred_element_type=F32), v2_ref[...]))
        v = v + (vf_ref[...].reshape(m, d) - v) * gate
    a = jax.nn.sigmoid(a0_ref[...] + _bdot(jnp.dot(xa, a1_ref[...], preferred_element_type=F32), a2_ref[...]))
    g = _bdot(jax.nn.sigmoid(jnp.dot(xg, g1_ref[...], preferred_element_type=F32)), g2_ref[...])
    kk = k * kk_ref[...]
    nrm = jnp.maximum(jnp.sqrt(_segsum(kk * kk, bd_ref[...], 3)), L2_EPS)
    kk = kk / nrm
    k = k * (1.0 + (a - 1.0) * ka_ref[...])

    shp = (bb, tt, d)
    r_o[...] = r.reshape(shp)
    lw_o[...] = lw.reshape(shp)
    k_o[...] = k.reshape(shp)
    v_o[...] = v.reshape(shp)
    kkn_o[...] = kk.reshape(shp)
    b_o[...] = (kk * a).reshape(shp)
    g_o[...] = g.reshape(shp)


def _rwkv_proj(x, shift0, g, p, ri, v_first, bd, *, bb, tt):
    b, t, d = x.shape
    vec = lambda a: a.reshape(1, d)
    tile = pl.BlockSpec((bb, tt, d), lambda i, j: (i, j, 0))
    row = pl.BlockSpec((bb, 1, d), lambda i, j: (i, 0, 0))
    args = [x, shift0.reshape(b, 1, d), vec(g), p['rw_mix'][ri],
            p['rw_w_r'][ri], p['rw_w_k'][ri], p['rw_w_v'][ri],
            vec(p['rw_w0'][ri]), p['rw_w1'][ri], p['rw_w2'][ri],
            vec(p['rw_a0'][ri]), p['rw_a1'][ri], p['rw_a2'][ri],
            p['rw_g1'][ri], p['rw_g2'][ri], vec(p['rw_k_k'][ri]), vec(p['rw_k_a'][ri]), bd]
    in_specs = [tile, row] + [_resident(a.shape) for a in args[2:]]
    has_vfirst = v_first is not None
    if has_vfirst:
        vi = ri - 1
        extra = [v_first, vec(p['rw_v0'][vi]), p['rw_v1'][vi], p['rw_v2'][vi]]
        args += extra
        in_specs += [tile] + [_resident(a.shape) for a in extra[1:]]
    big = jax.ShapeDtypeStruct((b, t, d), F32)
    outs = pl.pallas_call(
        functools.partial(_rwkv_proj_kernel, has_vfirst=has_vfirst),
        grid=(b // bb, t // tt),
        in_specs=in_specs,
        out_specs=[tile] * 7 + [row],
        out_shape=[big] * 7 + [jax.ShapeDtypeStruct((b, 1, d), F32)],
        scratch_shapes=[pltpu.VMEM((bb, tt + 8, d), F32)],
        compiler_params=_params(2),
        name="rwkv_proj",
    )(*args)
    return outs


def _scan_kernel(r_ref, lw_ref, k_ref, v_ref, kk_ref, b_ref, s0_ref, tri_ref, y_ref, so_ref, s_scr, *, tc, C):
    bb, _, d = r_ref.shape
    n = RWKV_HEAD
    n_pairs = d // LANES
    t = pl.program_id(1)

    @pl.when(t == 0)
    def _():
        s_scr[...] = s0_ref[...]

    def load(ref, i):
        x = ref[i]
        if tc < C:
            x = jnp.concatenate([x, jnp.zeros((C - tc, d), F32)], axis=0)
        return x

    lane_lo = lax.broadcasted_iota(jnp.int32, (1, LANES), 1) < n
    row = lax.broadcasted_iota(jnp.int32, (2 * C, 2 * C), 0)
    col = lax.broadcasted_iota(jnp.int32, (2 * C, 2 * C), 1)
    row_hi = row >= C
    col_hi = col >= C
    tr = jnp.where(row_hi, row - C, row)
    tj = jnp.where(col_hi, col - C, col)
    diag = row_hi == col_hi
    m_l = jnp.logical_and(diag, tj < tr)
    m_ak = jnp.logical_and(jnp.logical_not(diag), tj < tr)
    m_incl = tj <= tr
    eye = (row == col).astype(F32)
    rl_row = lax.broadcasted_iota(jnp.int32, (2 * C, LANES), 0) >= C
    rl_lane = lax.broadcasted_iota(jnp.int32, (2 * C, LANES), 1) >= n
    m_rl = rl_row == rl_lane

    def stack_masked(x):
        return jnp.concatenate([jnp.where(lane_lo, x, 0.0), jnp.where(lane_lo, 0.0, x)], axis=0).astype(BF16)

    units = []
    for i in range(bb):
        r, lw, k, v, kk, b = (load(ref, i) for ref in (r_ref, lw_ref, k_ref, v_ref, kk_ref, b_ref))
        cl = _split_dot_lhs(tri_ref[...], lw, 3)
        cl_end = cl[C - 1:C, :]
        e_neg = jnp.exp(-cl)
        e_end = jnp.exp(cl_end - cl)
        a_hat = -kk * jnp.exp(cl - lw)
        r_hat = r * jnp.exp(cl)
        b_hat = b * e_neg
        k_hat = k * e_neg
        b_bar = b * e_end
        k_bar = k * e_end
        w_end = jnp.exp(cl_end)
        for p in range(n_pairs):
            sl = slice(p * LANES, (p + 1) * LANES)
            units.append(dict(i=i, p=p, sl=sl, vv=v[:, sl], a_hat=a_hat[:, sl], r_hat=r_hat[:, sl],
                              b_hat=b_hat[:, sl], k_hat=k_hat[:, sl], b_bar=b_bar[:, sl], k_bar=k_bar[:, sl],
                              w_end=w_end[:, sl]))

    for u in units:
        bh, kh = u['b_hat'], u['k_hat']
        u['s_old'] = s_scr[u['i'], u['p']]
        rhs = jnp.concatenate([u['s_old'], u['s_old'], jnp.where(lane_lo, bh, kh), jnp.where(lane_lo, kh, bh)],
                              axis=0).astype(BF16)
        gp_a = lax.dot_general(stack_masked(u['a_hat']), rhs, NT, preferred_element_type=F32)
        gp_r = lax.dot_general(stack_masked(u['r_hat']), rhs, NT, preferred_element_type=F32)
        u['p_a'], g_a = gp_a[:, :LANES], gp_a[:, LANES:]
        u['p_r'], g_r = gp_r[:, :LANES], gp_r[:, LANES:]
        u['l_pow'] = jnp.where(m_l, g_a, 0.0)
        u['t_mat'] = eye + u['l_pow']
        u['ak'] = jnp.where(m_ak, g_a, 0.0)
        u['g_r'] = jnp.where(m_incl, g_r, 0.0)

    for _ in range(int(math.log2(C)) - 1):
        for u in units:
            u['l_pow'] = _bdot(u['l_pow'], u['l_pow'])
        for u in units:
            u['t_mat'] = u['t_mat'] + _bdot(u['t_mat'], u['l_pow'])

    for u in units:
        vv2 = jnp.concatenate([u['vv'], u['vv']], axis=0)
        u['rhs_u'] = jnp.where(m_rl, u['p_a'] + _bdot(u['ak'], vv2), 0.0)
    for u in units:
        u_st = _bdot(u['t_mat'], u['rhs_u'])
        u['u_lo'], u['u_hi'] = u_st[:C], u_st[C:]
    for u in units:
        vv = u['vv']
        w_mix = jnp.concatenate([jnp.where(lane_lo, u['u_lo'], vv), jnp.where(lane_lo, vv, u['u_hi'])], axis=0)
        y_st = jnp.where(m_rl, u['p_r'] + _bdot(u['g_r'], w_mix), 0.0)
        y = y_st[:C] + y_st[C:]
        y_ref[u['i'], :, u['sl']] = y[:tc]
    for u in units:
        uv_t = jnp.concatenate([u['u_lo'] + u['u_hi'], u['vv']], axis=0).T
        bk = jnp.concatenate([u['b_bar'], u['k_bar']], axis=0)
        full = _bdot(uv_t, bk)
        s_new = u['s_old'] * u['w_end'] + jnp.where(lane_lo, full[:n], full[n:])
        s_scr[u['i'], u['p']] = s_new
        so_ref[u['i'], u['p']] = s_new


def _rwkv_scan(r, lw, k, v, kk, b, s0_pair, tri, *, bb, tc):
    bsz, t, d = r.shape
    n_pairs = d // LANES
    chunk = tri.shape[0]
    tile = pl.BlockSpec((bb, tc, d), lambda i, j: (i, j, 0))
    st = pl.BlockSpec((bb, n_pairs, RWKV_HEAD, LANES), lambda i, j: (i, 0, 0, 0))
    return pl.pallas_call(
        functools.partial(_scan_kernel, tc=tc, C=chunk),
        grid=(bsz // bb, t // tc),
        in_specs=[tile] * 6 + [st, _resident(tri.shape)],
        out_specs=[tile, st],
        out_shape=[jax.ShapeDtypeStruct((bsz, t, d), F32),
                   jax.ShapeDtypeStruct((bsz, n_pairs, RWKV_HEAD, LANES), F32)],
        scratch_shapes=[pltpu.VMEM((bb, n_pairs, RWKV_HEAD, LANES), F32)],
        compiler_params=_params(2),
        name="rwkv_scan",
    )(r, lw, k, v, kk, b, s0_pair, tri)


def _rwkv_out_kernel(y_ref, r_ref, k_ref, v_ref, g_ref, x_ref, lg_ref, lb_ref, rk_ref, wo_ref, bd_ref, o_ref):
    bd = bd_ref[...]
    inv_n = 1.0 / RWKV_HEAD
    y = y_ref[...]
    yc = y - _segsum(y, bd, 2) * inv_n
    var = _segsum(yc * yc, bd, 2) * inv_n
    yn = yc * lax.rsqrt(var + LNX_EPS) * lg_ref[...] + lb_ref[...]
    v = v_ref[...]
    bonus = _segsum(r_ref[...] * k_ref[...] * rk_ref[...], bd, 2) * v
    z = ((yn + bonus) * g_ref[...]).astype(BF16)
    o_ref[...] = x_ref[...] + jnp.dot(z, wo_ref[...], preferred_element_type=F32)


def _rwkv_out(y, r, k, v, g, x, lnx_g, lnx_b, r_k, w_o, bd, *, tm=256):
    m, d = x.shape
    tm = min(tm, m)
    tile = pl.BlockSpec((tm, d), lambda i: (i, 0))
    vec = lambda a: a.reshape(1, d)
    return pl.pallas_call(
        _rwkv_out_kernel,
        grid=(m // tm,),
        in_specs=[tile] * 6 + [_resident((1, d))] * 3 + [_resident(w_o.shape), _resident(bd.shape)],
        out_specs=tile,
        out_shape=jax.ShapeDtypeStruct((m, d), F32),
        compiler_params=_params(1),
        name="rwkv_out",
    )(y, r, k, v, g, x, vec(lnx_g), vec(lnx_b), vec(r_k), w_o, bd)


def _to_pair(s):
    b, h, n, _ = s.shape
    return s.reshape(b, h // 2, 2, n, n).transpose(0, 1, 3, 2, 4).reshape(b, h // 2, n, 2 * n)


def _from_pair(sp):
    b, hp, n, _ = sp.shape
    return sp.reshape(b, hp, n, 2, n).transpose(0, 1, 3, 2, 4).reshape(b, 2 * hp, n, n)


def _trunk(x, mem_k, mem_v, conv_state, shift_state, wkv_state, p, consts, cfg):
    bsz, t, d = x.shape
    m = bsz * t
    depth = p['norm_g'].shape[0]
    bd = consts
    ar = jnp.arange(cfg['scan_chunk'])
    tri = (ar[None, :] <= ar[:, None]).astype(BF16)
    x2 = x.reshape(m, d)
    conv_new, shift_new, wkv_new = [], [], []
    v_first = None
    for layer in range(depth):
        g = p['norm_g'][layer]
        if layer % 2 == 0:
            ci = layer // 2
            u = _norm_matmul(x2, g[0], p['cv_w_in'][ci], p['cv_b_in'][ci], glu=True)
            z, cs = _conv_core(u.reshape(bsz, t, d), conv_state[ci], p['cv_w_dw'][ci], p['cv_b_dw'][ci],
                               p['cv_ln_g'][ci], p['cv_ln_b'][ci], bb=cfg['conv_bb'], tt=cfg['conv_tt'])
            conv_new.append(cs)
            x2 = _matmul_res(z.reshape(m, d), p['cv_w_out'][ci], x2, p['cv_b_out'][ci])
        else:
            ri = layer // 2
            r, lw, k, v, kk, b, gt, sh = _rwkv_proj(x2.reshape(bsz, t, d), shift_state[ri], g[0], p, ri, v_first, bd,
                                                    bb=cfg['proj_bb'], tt=cfg['proj_tt'])
            if v_first is None:
                v_first = v
            y, s_new = _rwkv_scan(r, lw, k, v, kk, b, _to_pair(wkv_state[ri]), tri,
                                  bb=cfg['scan_bb'], tc=cfg['scan_tc'])
            shift_new.append(sh.reshape(bsz, d))
            wkv_new.append(_from_pair(s_new))
            flat = lambda a: a.reshape(m, d)
            x2 = _rwkv_out(flat(y), flat(r), flat(k), flat(v), flat(gt), x2, p['rw_lnx_g'][ri], p['rw_lnx_b'][ri],
                           p['rw_r_k'][ri].reshape(d), p['rw_w_o'][ri], bd)
        q = _norm_matmul(x2, g[1], p['xa_w_q'][layer], out_dtype=cfg['attn_dtype'])
        o = _attn(q.reshape(bsz, t, d), mem_k, mem_v, layer, bb=cfg['attn_bb'], tt=cfg['attn_tt'],
                  out_dtype=cfg['attn_dtype'])
        x2 = _matmul_res(o.reshape(m, d), p['xa_w_o'][layer], x2)
        x2 = _mlp(x2, g[2], p['mlp_w_up'][layer], p['mlp_w_down'][layer], p['final_g'],
                  final_norm=layer == depth - 1)
    return x2.reshape(bsz, t, d), jnp.stack(conv_new), jnp.stack(shift_new), jnp.stack(wkv_new)


PROMPT_CFG = dict(conv_bb=1, conv_tt=128, proj_bb=1, proj_tt=256, scan_bb=1, scan_tc=SCAN_CHUNK,
                  scan_chunk=SCAN_CHUNK, attn_bb=1, attn_tt=512, attn_dtype=BF16)
SAMPLE_CFG = dict(conv_bb=16, conv_tt=8, proj_bb=16, proj_tt=8, scan_bb=4, scan_tc=8,
                  scan_chunk=8, attn_bb=4, attn_tt=8, attn_dtype=F32)


def kernel(x_prompt, x_sample, mem_prompt, cache_mem_k, cache_mem_v, state_conv, state_shift, state_wkv, norm_g, final_g, cv_w_in, cv_b_in, cv_w_dw, cv_b_dw, cv_ln_g, cv_ln_b, cv_w_out, cv_b_out, rw_mix, rw_w_r, rw_w_k, rw_w_v, rw_w_o, rw_w0, rw_w1, rw_w2, rw_a0, rw_a1, rw_a2, rw_v0, rw_v1, rw_v2, rw_g1, rw_g2, rw_k_k, rw_k_a, rw_r_k, rw_lnx_g, rw_lnx_b, xa_w_q, xa_w_kv, xa_w_o, mlp_w_up, mlp_w_down):
    bf = lambda w: w.astype(BF16)
    p = {
        'norm_g': norm_g, 'final_g': final_g,
        'cv_w_in': bf(cv_w_in), 'cv_b_in': cv_b_in, 'cv_w_dw': cv_w_dw, 'cv_b_dw': cv_b_dw,
        'cv_ln_g': cv_ln_g, 'cv_ln_b': cv_ln_b, 'cv_w_out': bf(cv_w_out), 'cv_b_out': cv_b_out,
        'rw_mix': rw_mix, 'rw_w_r': bf(rw_w_r), 'rw_w_k': bf(rw_w_k), 'rw_w_v': bf(rw_w_v), 'rw_w_o': bf(rw_w_o),
        'rw_w0': rw_w0, 'rw_w1': bf(rw_w1), 'rw_w2': bf(rw_w2), 'rw_a0': rw_a0, 'rw_a1': bf(rw_a1),
        'rw_a2': bf(rw_a2), 'rw_v0': rw_v0, 'rw_v1': bf(rw_v1), 'rw_v2': bf(rw_v2), 'rw_g1': bf(rw_g1),
        'rw_g2': bf(rw_g2), 'rw_k_k': rw_k_k, 'rw_k_a': rw_k_a, 'rw_r_k': rw_r_k,
        'rw_lnx_g': rw_lnx_g, 'rw_lnx_b': rw_lnx_b,
        'xa_w_q': bf(xa_w_q), 'xa_w_o': bf(xa_w_o), 'mlp_w_up': bf(mlp_w_up), 'mlp_w_down': bf(mlp_w_down),
    }
    depth = norm_g.shape[0]
    bp, mlen, d = mem_prompt.shape
    n_conv, n_rwkv = state_conv.shape[0], state_wkv.shape[0]
    heads, n = state_wkv.shape[2], state_wkv.shape[3]
    hd = d // XA_HEADS

    idx = jnp.arange(SEG_TILE) // RWKV_HEAD
    bd = (idx[:, None] == idx[None, :]).astype(BF16)
    consts = bd

    mem_k_prompt, mem_v_prompt = _memkv(mem_prompt, bf(xa_w_kv))

    conv0 = jnp.zeros((n_conv, bp, CONV_STATE, d), F32)
    shift0 = jnp.zeros((n_rwkv, bp, d), F32)
    wkv0 = jnp.zeros((n_rwkv, bp, heads, n, n), F32)
    y_prompt, conv_prompt, shift_prompt, wkv_prompt = _trunk(
        x_prompt, mem_k_prompt, mem_v_prompt, conv0, shift0, wkv0, p, consts, PROMPT_CFG)
    y_sample, conv_sample, shift_sample, wkv_sample = _trunk(
        x_sample, cache_mem_k, cache_mem_v, state_conv, state_shift, state_wkv, p, consts, SAMPLE_CFG)
    return (y_prompt, y_sample, mem_k_prompt, mem_v_prompt, conv_prompt, shift_prompt, wkv_prompt,
            conv_sample, shift_sample, wkv_sample)
```

```python
import functools
import math

import jax
import jax.numpy as jnp
from jax import lax
from jax.experimental import pallas as pl
from jax.experimental.pallas import tpu as pltpu

F32 = jnp.float32
BF16 = jnp.bfloat16

RMS_EPS = 1e-6
LN_EPS = 1e-5
L2_EPS = 1e-12
LNX_EPS = 64e-5
CONV_WIDTH = 31
CONV_STATE = CONV_WIDTH - 1
CONV_PAD = 32
RWKV_HEAD = 64
XA_HEADS = 4
SCAN_CHUNK = 64
LANES = 128
SUBLANES = 8
SEG_TILE = 256
VMEM_LIMIT = 56 * 1024 * 1024

NT = (((1,), (1,)), ((), ()))


def _params(n_axes, sem=None):
    if sem is None:
        sem = ("arbitrary",) * n_axes
    return pltpu.CompilerParams(dimension_semantics=sem, vmem_limit_bytes=VMEM_LIMIT)


def _resident(shape):
    nd = len(shape)
    return pl.BlockSpec(tuple(shape), lambda *_: (0,) * nd, pipeline_mode=pl.Buffered(1))


def _drop_ref(body, pos):
    def wrapped(*refs):
        return body(*refs[:pos], *refs[pos + 1:])
    return wrapped


def _rms(x, g):
    ms = jnp.mean(x * x, axis=-1, keepdims=True)
    return x * lax.rsqrt(ms + RMS_EPS) * g


def _bdot(a, b):
    return jnp.dot(a.astype(BF16), b.astype(BF16), preferred_element_type=F32)


def _split_dot_rhs(x, w, passes):
    acc = None
    rem = x
    for i in range(passes):
        part = rem.astype(BF16)
        term = jnp.dot(part, w, preferred_element_type=F32)
        acc = term if acc is None else acc + term
        if i + 1 < passes:
            rem = rem - part.astype(F32)
    return acc


def _split_dot_lhs(w, x, passes):
    acc = None
    rem = x
    for i in range(passes):
        part = rem.astype(BF16)
        term = jnp.dot(w, part, preferred_element_type=F32)
        acc = term if acc is None else acc + term
        if i + 1 < passes:
            rem = rem - part.astype(F32)
    return acc


def _segsum(x, bd, passes):
    d = x.shape[-1]
    outs = []
    for c in range(d // SEG_TILE):
        outs.append(_split_dot_rhs(x[:, c * SEG_TILE:(c + 1) * SEG_TILE], bd, passes))
    return jnp.concatenate(outs, axis=-1)


def _norm_matmul_kernel(*refs, glu, has_bias):
    if has_bias:
        x_ref, g_ref, w_ref, b_ref, o_ref = refs
    else:
        x_ref, g_ref, w_ref, o_ref = refs
    h = _rms(x_ref[...], g_ref[...]).astype(BF16)
    r = jnp.dot(h, w_ref[...], preferred_element_type=F32)
    if has_bias:
        r = r + b_ref[...]
    if glu:
        d = r.shape[-1] // 2
        r = r[:, :d] * jax.nn.sigmoid(r[:, d:])
    o_ref[...] = r.astype(o_ref.dtype)


def _norm_matmul(x, g, w, b=None, *, glu=False, out_dtype=F32, tm=512):
    m, d = x.shape
    n = w.shape[1]
    n_out = n // 2 if glu else n
    tm = min(tm, m)
    in_specs = [pl.BlockSpec((tm, d), lambda i: (i, 0)), _resident((1, d)), _resident((d, n))]
    args = [x, g.reshape(1, d), w]
    if b is not None:
        in_specs.append(_resident((1, n)))
        args.append(b.reshape(1, n))
    return pl.pallas_call(
        functools.partial(_norm_matmul_kernel, glu=glu, has_bias=b is not None),
        grid=(m // tm,),
        in_specs=in_specs,
        out_specs=pl.BlockSpec((tm, n_out), lambda i: (i, 0)),
        out_shape=jax.ShapeDtypeStruct((m, n_out), out_dtype),
        compiler_params=_params(1),
        name="norm_matmul",
    )(*args)


def _matmul_res_kernel(*refs, has_bias):
    if has_bias:
        a_ref, w_ref, b_ref, x_ref, o_ref = refs
    else:
        a_ref, w_ref, x_ref, o_ref = refs
    r = jnp.dot(a_ref[...].astype(BF16), w_ref[...], preferred_element_type=F32)
    if has_bias:
        r = r + b_ref[...]
    o_ref[...] = x_ref[...] + r


def _matmul_res(a, w, x, b=None, *, tm=512):
    m, k = a.shape
    d = w.shape[1]
    tm = min(tm, m)
    in_specs = [pl.BlockSpec((tm, k), lambda i: (i, 0)), _resident((k, d))]
    args = [a, w]
    if b is not None:
        in_specs.append(_resident((1, d)))
        args.append(b.reshape(1, d))
    in_specs.append(pl.BlockSpec((tm, d), lambda i: (i, 0)))
    args.append(x)
    return pl.pallas_call(
        functools.partial(_matmul_res_kernel, has_bias=b is not None),
        grid=(m // tm,),
        in_specs=in_specs,
        out_specs=pl.BlockSpec((tm, d), lambda i: (i, 0)),
        out_shape=jax.ShapeDtypeStruct((m, d), F32),
        compiler_params=_params(1),
        name="matmul_res",
    )(*args)


def _mlp_kernel(x_ref, g_ref, wu_ref, wd_ref, fg_ref, o_ref, *, final_norm, n_chunks):
    x = x_ref[...]
    h = _rms(x, g_ref[...]).astype(BF16)
    ck = wu_ref.shape[1] // n_chunks
    acc = x
    for c in range(n_chunks):
        u = jnp.dot(h, wu_ref[:, c * ck:(c + 1) * ck], preferred_element_type=F32)
        u = jnp.maximum(u, 0.0)
        u = (u * u).astype(BF16)
        acc = acc + jnp.dot(u, wd_ref[c * ck:(c + 1) * ck, :], preferred_element_type=F32)
    if final_norm:
        acc = _rms(acc, fg_ref[...])
    o_ref[...] = acc


def _mlp(x, g, w_up, w_down, final_g, *, final_norm, tm=512):
    m, d = x.shape
    dff = w_up.shape[1]
    tm = min(tm, m)
    return pl.pallas_call(
        functools.partial(_mlp_kernel, final_norm=final_norm, n_chunks=4),
        grid=(m // tm,),
        in_specs=[pl.BlockSpec((tm, d), lambda i: (i, 0)), _resident((1, d)),
                  _resident((d, dff)), _resident((dff, d)), _resident((1, d))],
        out_specs=pl.BlockSpec((tm, d), lambda i: (i, 0)),
        out_shape=jax.ShapeDtypeStruct((m, d), F32),
        compiler_params=_params(1),
        name="mlp",
    )(x, g.reshape(1, d), w_up, w_down, final_g.reshape(1, d))


def _memkv_kernel(x_ref, w_ref, k_ref, v_ref):
    r = jnp.dot(x_ref[...].astype(BF16), w_ref[0], preferred_element_type=F32)
    tb, mlen, heads, hd = k_ref.shape
    d = heads * hd
    for s in range(tb):
        rows = slice(s * mlen, (s + 1) * mlen)
        for h in range(heads):
            k_ref[s, :, h, :] = r[rows, h * hd:(h + 1) * hd]
            v_ref[s, :, h, :] = r[rows, d + h * hd:d + (h + 1) * hd]


def _memkv(mem, w_kv, *, tb=2):
    bp, mlen, d = mem.shape
    depth = w_kv.shape[0]
    hd = d // XA_HEADS
    out = jax.ShapeDtypeStruct((depth, bp, mlen, XA_HEADS, hd), F32)
    blk = pl.BlockSpec((None, tb, mlen, XA_HEADS, hd), lambda l, i: (l, i, 0, 0, 0))
    return pl.pallas_call(
        _memkv_kernel,
        grid=(depth, bp // tb),
        in_specs=[pl.BlockSpec((tb * mlen, d), lambda l, i: (i, 0)),
                  pl.BlockSpec((1, d, 2 * d), lambda l, i: (l, 0, 0))],
        out_specs=[blk, blk],
        out_shape=[out, out],
        compiler_params=_params(2),
        name="memkv",
    )(mem.reshape(bp * mlen, d), w_kv)


def _attn_kernel(q_ref, k_ref, v_ref, o_ref, *, scale):
    bb, tt, _ = q_ref.shape
    m, heads, hd = k_ref.shape[1], k_ref.shape[2], k_ref.shape[3]
    row_head = lax.broadcasted_iota(jnp.int32, (heads * tt, m * heads), 0) // tt
    col_head = lax.broadcasted_iota(jnp.int32, (heads * tt, m * heads), 1) % heads
    own = row_head == col_head
    for i in range(bb):
        k2 = k_ref[i].reshape(m * heads, hd).astype(BF16)
        v2 = v_ref[i].reshape(m * heads, hd).astype(BF16)
        q2 = jnp.concatenate([q_ref[i, :, h * hd:(h + 1) * hd] for h in range(heads)], axis=0).astype(BF16)
        s = lax.dot_general(q2, k2, NT, preferred_element_type=F32) * scale
        s = jnp.where(own, s, -jnp.inf)
        e = jnp.exp(s - jnp.max(s, axis=-1, keepdims=True))
        p = (e / jnp.sum(e, axis=-1, keepdims=True)).astype(BF16)
        o2 = jnp.dot(p, v2, preferred_element_type=F32)
        o_ref[i] = jnp.concatenate([o2[h * tt:(h + 1) * tt] for h in range(heads)], axis=-1).astype(o_ref.dtype)


def _attn(q, mem_k, mem_v, layer, *, bb, tt, out_dtype):
    b, t, d = q.shape
    _, _, m, heads, hd = mem_k.shape
    kv = pl.BlockSpec((None, bb, m, heads, hd), lambda i, j: (layer, i, 0, 0, 0))
    return pl.pallas_call(
        functools.partial(_attn_kernel, scale=hd ** -0.5),
        grid=(b // bb, t // tt),
        in_specs=[pl.BlockSpec((bb, tt, d), lambda i, j: (i, j, 0)), kv, kv],
        out_specs=pl.BlockSpec((bb, tt, d), lambda i, j: (i, j, 0)),
        out_shape=jax.ShapeDtypeStruct((b, t, d), out_dtype),
        compiler_params=_params(2),
        name="attn_core",
    )(q, mem_k, mem_v)


def _conv_kernel(u_ref, c0_ref, f_ref, bdw_ref, lg_ref, lb_ref, z_ref, cs_ref, up_ref, *, lane_chunk):
    bb, tt, d = u_ref.shape
    t = pl.program_id(1)
    lo = CONV_PAD - CONV_STATE

    @pl.when(t == 0)
    def _():
        up_ref[:, 0:lo, :] = jnp.zeros((bb, lo, d), F32)
        up_ref[:, CONV_PAD + tt:, :] = jnp.zeros((bb, SUBLANES, d), F32)
        up_ref[:, lo:CONV_PAD, :] = c0_ref[...]

    up_ref[:, CONV_PAD:CONV_PAD + tt, :] = u_ref[...]
    for i in range(bb):
        zs = []
        for c in range(d // lane_chunk):
            cs = slice(c * lane_chunk, (c + 1) * lane_chunk)
            acc = None
            for o in range(SUBLANES):
                g = None
                for j in range(CONV_WIDTH):
                    if (lo + j) % SUBLANES != o:
                        continue
                    base = lo + j - o
                    term = up_ref[i, base:base + tt + SUBLANES, cs] * f_ref[j:j + 1, cs]
                    g = term if g is None else g + term
                shifted = g[o:o + tt]
                acc = shifted if acc is None else acc + shifted
            zs.append(acc)
        z = jnp.concatenate(zs, axis=-1) + bdw_ref[...]
        zc = z - jnp.mean(z, axis=-1, keepdims=True)
        zn = zc * lax.rsqrt(jnp.mean(zc * zc, axis=-1, keepdims=True) + LN_EPS)
        zn = zn * lg_ref[...] + lb_ref[...]
        z_ref[i] = (zn * jax.nn.sigmoid(zn)).astype(z_ref.dtype)
    tail = up_ref[:, tt + lo:tt + CONV_PAD, :]
    cs_ref[...] = tail
    up_ref[:, lo:CONV_PAD, :] = tail


def _conv_core(u, conv_all, ci, carry, filt, b_dw, ln_g, ln_b, *, bb, tt):
    b, t, d = u.shape
    vec = lambda a: a.reshape(1, d)
    slab = pl.BlockSpec((None, bb, CONV_STATE, d), lambda i, j: (ci, i, 0, 0))
    args = [u, conv_all, filt, vec(b_dw), vec(ln_g), vec(ln_b)]
    in_specs = [pl.BlockSpec((bb, tt, d), lambda i, j: (i, j, 0)), slab,
                _resident((CONV_WIDTH, d)), _resident((1, d)), _resident((1, d)), _resident((1, d))]
    body = functools.partial(_conv_kernel, lane_chunk=2 * LANES)
    aliases = {}
    if carry is not None:
        args.append(carry)
        in_specs.append(pl.BlockSpec(memory_space=pl.ANY))
        aliases = {len(args) - 1: 1}
        body = _drop_ref(body, len(args) - 1)
    return pl.pallas_call(
        body,
        grid=(b // bb, t // tt),
        in_specs=in_specs,
        out_specs=[pl.BlockSpec((bb, tt, d), lambda i, j: (i, j, 0)), slab],
        out_shape=[jax.ShapeDtypeStruct((b, t, d), F32), jax.ShapeDtypeStruct(conv_all.shape, F32)],
        scratch_shapes=[pltpu.VMEM((bb, CONV_PAD + tt + SUBLANES, d), F32)],
        input_output_aliases=aliases,
        compiler_params=_params(2),
        name="conv_core",
    )(*args)


def _rwkv_proj_kernel(*refs, has_vfirst):
    (x_ref, sh0_ref, g_ref, mix_ref, wr_ref, wk_ref, wv_ref, w0_ref, w1_ref, w2_ref,
     a0_ref, a1_ref, a2_ref, g1_ref, g2_ref, kk_ref, ka_ref, bd_ref) = refs[:18]
    pos = 18
    if has_vfirst:
        vf_ref, v0_ref, v1_ref, v2_ref = refs[pos:pos + 4]
        pos += 4
    (r_o, lw_o, k_o, v_o, kkn_o, b_o, g_o, sh_o) = refs[pos:pos + 8]
    hs_ref = refs[pos + 8]

    bb, tt, d = x_ref.shape
    t = pl.program_id(1)
    h3 = _rms(x_ref[...], g_ref[...])

    @pl.when(t == 0)
    def _():
        hs_ref[:, 7:8, :] = sh0_ref[...]

    hs_ref[:, 8:8 + tt, :] = h3
    prev3 = hs_ref[:, 7:7 + tt, :]
    last = h3[:, tt - 1:tt, :]
    hs_ref[:, 7:8, :] = last
    sh_o[...] = last

    m = bb * tt
    h = h3.reshape(m, d)
    xx = prev3.reshape(m, d) - h
    mix = mix_ref[...]
    xr, xw, xk, xv, xa, xg = ((h + xx * mix[i:i + 1, :]).astype(BF16) for i in range(6))

    r = jnp.dot(xr, wr_ref[...], preferred_element_type=F32)
    k = jnp.dot(xk, wk_ref[...], preferred_element_type=F32)
    v = jnp.dot(xv, wv_ref[...], preferred_element_type=F32)
    w_pre = w0_ref[...] + _bdot(jnp.tanh(jnp.dot(xw, w1_ref[...], preferred_element_type=F32)), w2_ref[...])
    lw = -math.exp(-0.5) * jax.nn.sigmoid(w_pre)
    if has_vfirst:
        gate = jax.nn.sigmoid(v0_ref[...] + _bdot(jnp.dot(xv, v1_ref[...], preferred_element_type=F32), v2_ref[...]))
        v = v + (vf_ref[...].reshape(m, d) - v) * gate
    a = jax.nn.sigmoid(a0_ref[...] + _bdot(jnp.dot(xa, a1_ref[...], preferred_element_type=F32), a2_ref[...]))
    g = _bdot(jax.nn.sigmoid(jnp.dot(xg, g1_ref[...], preferred_element_type=F32)), g2_ref[...])
    kk = k * kk_ref[...]
    nrm = jnp.maximum(jnp.sqrt(_segsum(kk * kk, bd_ref[...], 3)), L2_EPS)
    kk = kk / nrm
    k = k * (1.0 + (a - 1.0) * ka_ref[...])

    shp = (bb, tt, d)
    r_o[...] = r.reshape(shp)
    lw_o[...] = lw.reshape(shp)
    k_o[...] = k.reshape(shp)
    v_o[...] = v.reshape(shp)
    kkn_o[...] = kk.reshape(shp)
    b_o[...] = (kk * a).reshape(shp)
    g_o[...] = g.reshape(shp)


def _rwkv_proj(x, shift0, g, p, ri, v_first, bd, *, bb, tt):
    b, t, d = x.shape
    vec = lambda a: a.reshape(1, d)
    tile = pl.BlockSpec((bb, tt, d), lambda i, j: (i, j, 0))
    row = pl.BlockSpec((bb, 1, d), lambda i, j: (i, 0, 0))
    args = [x, shift0.reshape(b, 1, d), vec(g), p['rw_mix'][ri],
            p['rw_w_r'][ri], p['rw_w_k'][ri], p['rw_w_v'][ri],
            vec(p['rw_w0'][ri]), p['rw_w1'][ri], p['rw_w2'][ri],
            vec(p['rw_a0'][ri]), p['rw_a1'][ri], p['rw_a2'][ri],
            p['rw_g1'][ri], p['rw_g2'][ri], vec(p['rw_k_k'][ri]), vec(p['rw_k_a'][ri]), bd]
    in_specs = [tile, row] + [_resident(a.shape) for a in args[2:]]
    has_vfirst = v_first is not None
    if has_vfirst:
        vi = ri - 1
        extra = [v_first, vec(p['rw_v0'][vi]), p['rw_v1'][vi], p['rw_v2'][vi]]
        args += extra
        in_specs += [tile] + [_resident(a.shape) for a in extra[1:]]
    big = jax.ShapeDtypeStruct((b, t, d), F32)
    outs = pl.pallas_call(
        functools.partial(_rwkv_proj_kernel, has_vfirst=has_vfirst),
        grid=(b // bb, t // tt),
        in_specs=in_specs,
        out_specs=[tile] * 7 + [row],
        out_shape=[big] * 7 + [jax.ShapeDtypeStruct((b, 1, d), F32)],
        scratch_shapes=[pltpu.VMEM((bb, tt + 8, d), F32)],
        compiler_params=_params(2),
        name="rwkv_proj",
    )(*args)
    return outs


def _scan_kernel(r_ref, lw_ref, k_ref, v_ref, kk_ref, b_ref, s0_ref, tri_ref, y_ref, so_ref, s_scr, *, tc, C):
    bb, _, d = r_ref.shape
    n = RWKV_HEAD
    n_pairs = d // LANES
    t = pl.program_id(1)

    @pl.when(t == 0)
    def _():
        for i in range(bb):
            for p in range(n_pairs):
                s_scr[i, p] = jnp.concatenate([s0_ref[i, 2 * p], s0_ref[i, 2 * p + 1]], axis=-1)

    def load(ref, i):
        x = ref[i]
        if tc < C:
            x = jnp.concatenate([x, jnp.zeros((C - tc, d), F32)], axis=0)
        return x

    lane_lo = lax.broadcasted_iota(jnp.int32, (1, LANES), 1) < n
    row = lax.broadcasted_iota(jnp.int32, (2 * C, 2 * C), 0)
    col = lax.broadcasted_iota(jnp.int32, (2 * C, 2 * C), 1)
    row_hi = row >= C
    col_hi = col >= C
    tr = jnp.where(row_hi, row - C, row)
    tj = jnp.where(col_hi, col - C, col)
    diag = row_hi == col_hi
    m_l = jnp.logical_and(diag, tj < tr)
    m_ak = jnp.logical_and(jnp.logical_not(diag), tj < tr)
    m_incl = tj <= tr
    eye = (row == col).astype(F32)
    rl_row = lax.broadcasted_iota(jnp.int32, (2 * C, LANES), 0) >= C
    rl_lane = lax.broadcasted_iota(jnp.int32, (2 * C, LANES), 1) >= n
    m_rl = rl_row == rl_lane

    def stack_masked(x):
        return jnp.concatenate([jnp.where(lane_lo, x, 0.0), jnp.where(lane_lo, 0.0, x)], axis=0).astype(BF16)

    units = []
    for i in range(bb):
        r, lw, k, v, kk, b = (load(ref, i) for ref in (r_ref, lw_ref, k_ref, v_ref, kk_ref, b_ref))
        cl = _split_dot_lhs(tri_ref[...], lw, 3)
        cl_end = cl[C - 1:C, :]
        e_neg = jnp.exp(-cl)
        e_end = jnp.exp(cl_end - cl)
        a_hat = -kk * jnp.exp(cl - lw)
        r_hat = r * jnp.exp(cl)
        b_hat = b * e_neg
        k_hat = k * e_neg
        b_bar = b * e_end
        k_bar = k * e_end
        w_end = jnp.exp(cl_end)
        for p in range(n_pairs):
            sl = slice(p * LANES, (p + 1) * LANES)
            units.append(dict(i=i, p=p, sl=sl, vv=v[:, sl], a_hat=a_hat[:, sl], r_hat=r_hat[:, sl],
                              b_hat=b_hat[:, sl], k_hat=k_hat[:, sl], b_bar=b_bar[:, sl], k_bar=k_bar[:, sl],
                              w_end=w_end[:, sl]))

    for u in units:
        bh, kh = u['b_hat'], u['k_hat']
        u['s_old'] = s_scr[u['i'], u['p']]
        rhs = jnp.concatenate([u['s_old'], u['s_old'], jnp.where(lane_lo, bh, kh), jnp.where(lane_lo, kh, bh)],
                              axis=0).astype(BF16)
        lhs = jnp.concatenate([stack_masked(u['a_hat']), stack_masked(u['r_hat'])], axis=0)
        gp = lax.dot_general(lhs, rhs, NT, preferred_element_type=F32)
        u['p_a'], g_a = gp[:2 * C, :LANES], gp[:2 * C, LANES:]
        u['p_r'], g_r = gp[2 * C:, :LANES], gp[2 * C:, LANES:]
        u['l_mat'] = jnp.where(m_l, g_a, 0.0)
        u['t_mat'] = eye + u['l_mat']
        u['ak'] = jnp.where(m_ak, g_a, 0.0)
        u['g_r'] = jnp.where(m_incl, g_r, 0.0)

    n_factors = int(math.log2(C)) - 1
    for u in units:
        u['l_pow'] = _bdot(u['l_mat'], u['l_mat'])
    for _ in range(n_factors - 1):
        for u in units:
            both = _bdot(jnp.concatenate([u['t_mat'], u['l_pow']], axis=0), u['l_pow'])
            u['t_mat'] = u['t_mat'] + both[:2 * C]
            u['l_pow'] = both[2 * C:]
    for u in units:
        u['t_mat'] = u['t_mat'] + _bdot(u['t_mat'], u['l_pow'])

    for u in units:
        vv2 = jnp.concatenate([u['vv'], u['vv']], axis=0)
        u['rhs_u'] = jnp.where(m_rl, u['p_a'] + _bdot(u['ak'], vv2), 0.0)
    for u in units:
        u_st = _bdot(u['t_mat'], u['rhs_u'])
        u['u_lo'], u['u_hi'] = u_st[:C], u_st[C:]
    for u in units:
        vv = u['vv']
        w_mix = jnp.concatenate([jnp.where(lane_lo, u['u_lo'], vv), jnp.where(lane_lo, vv, u['u_hi'])], axis=0)
        y_st = jnp.where(m_rl, u['p_r'] + _bdot(u['g_r'], w_mix), 0.0)
        y = y_st[:C] + y_st[C:]
        y_ref[u['i'], :, u['sl']] = y[:tc]
    for u in units:
        uv_t = jnp.concatenate([u['u_lo'] + u['u_hi'], u['vv']], axis=0).T
        bk = jnp.concatenate([u['b_bar'], u['k_bar']], axis=0)
        full = _bdot(uv_t, bk)
        s_new = u['s_old'] * u['w_end'] + jnp.where(lane_lo, full[:n], full[n:])
        s_scr[u['i'], u['p']] = s_new

    @pl.when(t == pl.num_programs(1) - 1)
    def _():
        for i in range(bb):
            for p in range(n_pairs):
                s_fin = s_scr[i, p]
                so_ref[i, 2 * p] = s_fin[:, :n]
                so_ref[i, 2 * p + 1] = s_fin[:, n:]


def _rwkv_scan(r, lw, k, v, kk, b, s_all, ri, carry, tri, *, bb, tc):
    bsz, t, d = r.shape
    n_pairs = d // LANES
    chunk = tri.shape[0]
    tile = pl.BlockSpec((bb, tc, d), lambda i, j: (i, j, 0))
    st = pl.BlockSpec((None, bb) + s_all.shape[2:], lambda i, j: (ri, i, 0, 0, 0))
    args = [r, lw, k, v, kk, b, s_all, tri]
    in_specs = [tile] * 6 + [st, _resident(tri.shape)]
    body = functools.partial(_scan_kernel, tc=tc, C=chunk)
    aliases = {}
    if carry is not None:
        args.append(carry)
        in_specs.append(pl.BlockSpec(memory_space=pl.ANY))
        aliases = {len(args) - 1: 1}
        body = _drop_ref(body, len(args) - 1)
    return pl.pallas_call(
        body,
        grid=(bsz // bb, t // tc),
        in_specs=in_specs,
        out_specs=[tile, st],
        out_shape=[jax.ShapeDtypeStruct((bsz, t, d), F32), jax.ShapeDtypeStruct(s_all.shape, F32)],
        scratch_shapes=[pltpu.VMEM((bb, n_pairs, RWKV_HEAD, LANES), F32)],
        input_output_aliases=aliases,
        compiler_params=_params(2),
        name="rwkv_scan",
    )(*args)


def _rwkv_out_kernel(y_ref, r_ref, k_ref, v_ref, g_ref, x_ref, lg_ref, lb_ref, rk_ref, wo_ref, bd_ref, o_ref):
    bd = bd_ref[...]
    inv_n = 1.0 / RWKV_HEAD
    y = y_ref[...]
    yc = y - _segsum(y, bd, 2) * inv_n
    var = _segsum(yc * yc, bd, 2) * inv_n
    yn = yc * lax.rsqrt(var + LNX_EPS) * lg_ref[...] + lb_ref[...]
    v = v_ref[...]
    bonus = _segsum(r_ref[...] * k_ref[...] * rk_ref[...], bd, 2) * v
    z = ((yn + bonus) * g_ref[...]).astype(BF16)
    o_ref[...] = x_ref[...] + jnp.dot(z, wo_ref[...], preferred_element_type=F32)


def _rwkv_out(y, r, k, v, g, x, lnx_g, lnx_b, r_k, w_o, bd, *, tm=256):
    m, d = x.shape
    tm = min(tm, m)
    tile = pl.BlockSpec((tm, d), lambda i: (i, 0))
    vec = lambda a: a.reshape(1, d)
    return pl.pallas_call(
        _rwkv_out_kernel,
        grid=(m // tm,),
        in_specs=[tile] * 6 + [_resident((1, d))] * 3 + [_resident(w_o.shape), _resident(bd.shape)],
        out_specs=tile,
        out_shape=jax.ShapeDtypeStruct((m, d), F32),
        compiler_params=_params(1),
        name="rwkv_out",
    )(y, r, k, v, g, x, vec(lnx_g), vec(lnx_b), vec(r_k), w_o, bd)


def _trunk(x, mem_k, mem_v, conv_state, shift_state, wkv_state, p, consts, cfg):
    bsz, t, d = x.shape
    m = bsz * t
    depth = p['norm_g'].shape[0]
    bd = consts
    ar = jnp.arange(cfg['scan_chunk'])
    tri = (ar[None, :] <= ar[:, None]).astype(BF16)
    x2 = x.reshape(m, d)
    conv_new, wkv_new, shift_new = None, None, []
    v_first = None
    for layer in range(depth):
        g = p['norm_g'][layer]
        if layer % 2 == 0:
            ci = layer // 2
            u = _norm_matmul(x2, g[0], p['cv_w_in'][ci], p['cv_b_in'][ci], glu=True)
            z, conv_new = _conv_core(u.reshape(bsz, t, d), conv_state, ci, conv_new, p['cv_w_dw'][ci],
                                     p['cv_b_dw'][ci], p['cv_ln_g'][ci], p['cv_ln_b'][ci],
                                     bb=cfg['conv_bb'], tt=cfg['conv_tt'])
            x2 = _matmul_res(z.reshape(m, d), p['cv_w_out'][ci], x2, p['cv_b_out'][ci])
        else:
            ri = layer // 2
            r, lw, k, v, kk, b, gt, sh = _rwkv_proj(x2.reshape(bsz, t, d), shift_state[ri], g[0], p, ri, v_first, bd,
                                                    bb=cfg['proj_bb'], tt=cfg['proj_tt'])
            if v_first is None:
                v_first = v
            y, wkv_new = _rwkv_scan(r, lw, k, v, kk, b, wkv_state, ri, wkv_new, tri,
                                    bb=cfg['scan_bb'], tc=cfg['scan_tc'])
            shift_new.append(sh.reshape(bsz, d))
            flat = lambda a: a.reshape(m, d)
            x2 = _rwkv_out(flat(y), flat(r), flat(k), flat(v), flat(gt), x2, p['rw_lnx_g'][ri], p['rw_lnx_b'][ri],
                           p['rw_r_k'][ri].reshape(d), p['rw_w_o'][ri], bd)
        q = _norm_matmul(x2, g[1], p['xa_w_q'][layer], out_dtype=cfg['attn_dtype'])
        o = _attn(q.reshape(bsz, t, d), mem_k, mem_v, layer, bb=cfg['attn_bb'], tt=cfg['attn_tt'],
                  out_dtype=cfg['attn_dtype'])
        x2 = _matmul_res(o.reshape(m, d), p['xa_w_o'][layer], x2)
        x2 = _mlp(x2, g[2], p['mlp_w_up'][layer], p['mlp_w_down'][layer], p['final_g'],
                  final_norm=layer == depth - 1)
    return x2.reshape(bsz, t, d), conv_new, jnp.stack(shift_new), wkv_new


PROMPT_CFG = dict(conv_bb=1, conv_tt=128, proj_bb=1, proj_tt=256, scan_bb=1, scan_tc=SCAN_CHUNK,
                  scan_chunk=SCAN_CHUNK, attn_bb=1, attn_tt=512, attn_dtype=BF16)
SAMPLE_CFG = dict(conv_bb=16, conv_tt=8, proj_bb=16, proj_tt=8, scan_bb=4, scan_tc=8,
                  scan_chunk=8, attn_bb=4, attn_tt=8, attn_dtype=F32)


def kernel(x_prompt, x_sample, mem_prompt, cache_mem_k, cache_mem_v, state_conv, state_shift, state_wkv, norm_g, final_g, cv_w_in, cv_b_in, cv_w_dw, cv_b_dw, cv_ln_g, cv_ln_b, cv_w_out, cv_b_out, rw_mix, rw_w_r, rw_w_k, rw_w_v, rw_w_o, rw_w0, rw_w1, rw_w2, rw_a0, rw_a1, rw_a2, rw_v0, rw_v1, rw_v2, rw_g1, rw_g2, rw_k_k, rw_k_a, rw_r_k, rw_lnx_g, rw_lnx_b, xa_w_q, xa_w_kv, xa_w_o, mlp_w_up, mlp_w_down):
    bf = lambda w: w.astype(BF16)
    p = {
        'norm_g': norm_g, 'final_g': final_g,
        'cv_w_in': bf(cv_w_in), 'cv_b_in': cv_b_in, 'cv_w_dw': cv_w_dw, 'cv_b_dw': cv_b_dw,
        'cv_ln_g': cv_ln_g, 'cv_ln_b': cv_ln_b, 'cv_w_out': bf(cv_w_out), 'cv_b_out': cv_b_out,
        'rw_mix': rw_mix, 'rw_w_r': bf(rw_w_r), 'rw_w_k': bf(rw_w_k), 'rw_w_v': bf(rw_w_v), 'rw_w_o': bf(rw_w_o),
        'rw_w0': rw_w0, 'rw_w1': bf(rw_w1), 'rw_w2': bf(rw_w2), 'rw_a0': rw_a0, 'rw_a1': bf(rw_a1),
        'rw_a2': bf(rw_a2), 'rw_v0': rw_v0, 'rw_v1': bf(rw_v1), 'rw_v2': bf(rw_v2), 'rw_g1': bf(rw_g1),
        'rw_g2': bf(rw_g2), 'rw_k_k': rw_k_k, 'rw_k_a': rw_k_a, 'rw_r_k': rw_r_k,
        'rw_lnx_g': rw_lnx_g, 'rw_lnx_b': rw_lnx_b,
        'xa_w_q': bf(xa_w_q), 'xa_w_o': bf(xa_w_o), 'mlp_w_up': bf(mlp_w_up), 'mlp_w_down': bf(mlp_w_down),
    }
    depth = norm_g.shape[0]
    bp, mlen, d = mem_prompt.shape
    n_conv, n_rwkv = state_conv.shape[0], state_wkv.shape[0]
    heads, n = state_wkv.shape[2], state_wkv.shape[3]
    hd = d // XA_HEADS

    idx = jnp.arange(SEG_TILE) // RWKV_HEAD
    bd = (idx[:, None] == idx[None, :]).astype(BF16)
    consts = bd

    mem_k_prompt, mem_v_prompt = _memkv(mem_prompt, bf(xa_w_kv))

    conv0 = jnp.zeros((n_conv, bp, CONV_STATE, d), F32)
    shift0 = jnp.zeros((n_rwkv, bp, d), F32)
    wkv0 = jnp.zeros((n_rwkv, bp, heads, n, n), F32)
    y_prompt, conv_prompt, shift_prompt, wkv_prompt = _trunk(
        x_prompt, mem_k_prompt, mem_v_prompt, conv0, shift0, wkv0, p, consts, PROMPT_CFG)
    y_sample, conv_sample, shift_sample, wkv_sample = _trunk(
        x_sample, cache_mem_k, cache_mem_v, state_conv, state_shift, state_wkv, p, consts, SAMPLE_CFG)
    return (y_prompt, y_sample, mem_k_prompt, mem_v_prompt, conv_prompt, shift_prompt, wkv_prompt,
            conv_sample, shift_sample, wkv_sample)
```

```python
import functools
import math

import jax
import jax.numpy as jnp
from jax import lax
from jax.experimental import pallas as pl
from jax.experimental.pallas import tpu as pltpu

F32 = jnp.float32
BF16 = jnp.bfloat16

RMS_EPS = 1e-6
LN_EPS = 1e-5
L2_EPS = 1e-12
LNX_EPS = 64e-5
CONV_WIDTH = 31
CONV_STATE = CONV_WIDTH - 1
CONV_PAD = 32
RWKV_HEAD = 64
XA_HEADS = 4
SCAN_CHUNK = 64
LANES = 128
SUBLANES = 8
SEG_TILE = 256
VMEM_LIMIT = 56 * 1024 * 1024

NT = (((1,), (1,)), ((), ()))


def _params(n_axes, sem=None):
    if sem is None:
        sem = ("arbitrary",) * n_axes
    return pltpu.CompilerParams(dimension_semantics=sem, vmem_limit_bytes=VMEM_LIMIT)


def _resident(shape):
    nd = len(shape)
    return pl.BlockSpec(tuple(shape), lambda *_: (0,) * nd, pipeline_mode=pl.Buffered(1))


def _drop_ref(body, pos):
    def wrapped(*refs):
        return body(*refs[:pos], *refs[pos + 1:])
    return wrapped


def _rms(x, g):
    ms = jnp.mean(x * x, axis=-1, keepdims=True)
    return x * lax.rsqrt(ms + RMS_EPS) * g


def _bdot(a, b):
    return jnp.dot(a.astype(BF16), b.astype(BF16), preferred_element_type=F32)


def _split_dot_rhs(x, w, passes):
    acc = None
    rem = x
    for i in range(passes):
        part = rem.astype(BF16)
        term = jnp.dot(part, w, preferred_element_type=F32)
        acc = term if acc is None else acc + term
        if i + 1 < passes:
            rem = rem - part.astype(F32)
    return acc


def _split_dot_lhs(w, x, passes):
    acc = None
    rem = x
    for i in range(passes):
        part = rem.astype(BF16)
        term = jnp.dot(w, part, preferred_element_type=F32)
        acc = term if acc is None else acc + term
        if i + 1 < passes:
            rem = rem - part.astype(F32)
    return acc


def _segsum(x, bd, passes):
    d = x.shape[-1]
    outs = []
    for c in range(d // SEG_TILE):
        outs.append(_split_dot_rhs(x[:, c * SEG_TILE:(c + 1) * SEG_TILE], bd, passes))
    return jnp.concatenate(outs, axis=-1)


def _norm_matmul_kernel(*refs, glu, has_bias):
    if has_bias:
        x_ref, g_ref, w_ref, b_ref, o_ref = refs
    else:
        x_ref, g_ref, w_ref, o_ref = refs
    h = _rms(x_ref[...], g_ref[...]).astype(BF16)
    r = jnp.dot(h, w_ref[...], preferred_element_type=F32)
    if has_bias:
        r = r + b_ref[...]
    if glu:
        d = r.shape[-1] // 2
        r = r[:, :d] * jax.nn.sigmoid(r[:, d:])
    o_ref[...] = r.astype(o_ref.dtype)


def _norm_matmul(x, g, w, b=None, *, glu=False, out_dtype=F32, tm=512):
    m, d = x.shape
    n = w.shape[1]
    n_out = n // 2 if glu else n
    tm = min(tm, m)
    in_specs = [pl.BlockSpec((tm, d), lambda i: (i, 0)), _resident((1, d)), _resident((d, n))]
    args = [x, g.reshape(1, d), w]
    if b is not None:
        in_specs.append(_resident((1, n)))
        args.append(b.reshape(1, n))
    return pl.pallas_call(
        functools.partial(_norm_matmul_kernel, glu=glu, has_bias=b is not None),
        grid=(m // tm,),
        in_specs=in_specs,
        out_specs=pl.BlockSpec((tm, n_out), lambda i: (i, 0)),
        out_shape=jax.ShapeDtypeStruct((m, n_out), out_dtype),
        compiler_params=_params(1),
        name="norm_matmul",
    )(*args)


def _matmul_res_kernel(*refs, has_bias):
    if has_bias:
        a_ref, w_ref, b_ref, x_ref, o_ref = refs
    else:
        a_ref, w_ref, x_ref, o_ref = refs
    r = jnp.dot(a_ref[...].astype(BF16), w_ref[...], preferred_element_type=F32)
    if has_bias:
        r = r + b_ref[...]
    o_ref[...] = x_ref[...] + r


def _matmul_res(a, w, x, b=None, *, tm=512):
    m, k = a.shape
    d = w.shape[1]
    tm = min(tm, m)
    in_specs = [pl.BlockSpec((tm, k), lambda i: (i, 0)), _resident((k, d))]
    args = [a, w]
    if b is not None:
        in_specs.append(_resident((1, d)))
        args.append(b.reshape(1, d))
    in_specs.append(pl.BlockSpec((tm, d), lambda i: (i, 0)))
    args.append(x)
    return pl.pallas_call(
        functools.partial(_matmul_res_kernel, has_bias=b is not None),
        grid=(m // tm,),
        in_specs=in_specs,
        out_specs=pl.BlockSpec((tm, d), lambda i: (i, 0)),
        out_shape=jax.ShapeDtypeStruct((m, d), F32),
        compiler_params=_params(1),
        name="matmul_res",
    )(*args)


def _mlp_kernel(x_ref, g_ref, wu_ref, wd_ref, fg_ref, o_ref, *, final_norm, n_chunks):
    x = x_ref[...]
    h = _rms(x, g_ref[...]).astype(BF16)
    ck = wu_ref.shape[1] // n_chunks
    acc = x
    for c in range(n_chunks):
        u = jnp.dot(h, wu_ref[:, c * ck:(c + 1) * ck], preferred_element_type=F32)
        u = jnp.maximum(u, 0.0)
        u = (u * u).astype(BF16)
        acc = acc + jnp.dot(u, wd_ref[c * ck:(c + 1) * ck, :], preferred_element_type=F32)
    if final_norm:
        acc = _rms(acc, fg_ref[...])
    o_ref[...] = acc


def _mlp(x, g, w_up, w_down, final_g, *, final_norm, tm=512):
    m, d = x.shape
    dff = w_up.shape[1]
    tm = min(tm, m)
    return pl.pallas_call(
        functools.partial(_mlp_kernel, final_norm=final_norm, n_chunks=4),
        grid=(m // tm,),
        in_specs=[pl.BlockSpec((tm, d), lambda i: (i, 0)), _resident((1, d)),
                  _resident((d, dff)), _resident((dff, d)), _resident((1, d))],
        out_specs=pl.BlockSpec((tm, d), lambda i: (i, 0)),
        out_shape=jax.ShapeDtypeStruct((m, d), F32),
        compiler_params=_params(1),
        name="mlp",
    )(x, g.reshape(1, d), w_up, w_down, final_g.reshape(1, d))


def _memkv_kernel(x_ref, w_ref, k_ref, v_ref, kh_ref, vh_ref):
    r = jnp.dot(x_ref[...].astype(BF16), w_ref[0], preferred_element_type=F32)
    tb, mlen, heads, hd = k_ref.shape
    d = heads * hd
    for s in range(tb):
        rows = slice(s * mlen, (s + 1) * mlen)
        for h in range(heads):
            kb = r[rows, h * hd:(h + 1) * hd]
            vb = r[rows, d + h * hd:d + (h + 1) * hd]
            k_ref[s, :, h, :] = kb
            v_ref[s, :, h, :] = vb
            kh_ref[s, h] = kb.astype(BF16)
            vh_ref[s, h] = vb.astype(BF16)


def _memkv(mem, w_kv, *, tb=2):
    bp, mlen, d = mem.shape
    depth = w_kv.shape[0]
    hd = d // XA_HEADS
    out = jax.ShapeDtypeStruct((depth, bp, mlen, XA_HEADS, hd), F32)
    out_h = jax.ShapeDtypeStruct((depth, bp, XA_HEADS, mlen, hd), BF16)
    blk = pl.BlockSpec((None, tb, mlen, XA_HEADS, hd), lambda l, i: (l, i, 0, 0, 0))
    blk_h = pl.BlockSpec((None, tb, XA_HEADS, mlen, hd), lambda l, i: (l, i, 0, 0, 0))
    return pl.pallas_call(
        _memkv_kernel,
        grid=(depth, bp // tb),
        in_specs=[pl.BlockSpec((tb * mlen, d), lambda l, i: (i, 0)),
                  pl.BlockSpec((1, d, 2 * d), lambda l, i: (l, 0, 0))],
        out_specs=[blk, blk, blk_h, blk_h],
        out_shape=[out, out, out_h, out_h],
        compiler_params=_params(2),
        name="memkv",
    )(mem.reshape(bp * mlen, d), w_kv)


def _xattn_kernel(x_ref, g_ref, wq_ref, k_ref, v_ref, wo_ref, o_ref, *, scale):
    heads, _, hd = k_ref.shape
    x = x_ref[...]
    q = jnp.dot(_rms(x, g_ref[...]).astype(BF16), wq_ref[...], preferred_element_type=F32).astype(BF16)
    outs = []
    for h in range(heads):
        s = lax.dot_general(q[:, h * hd:(h + 1) * hd], k_ref[h], NT, preferred_element_type=F32) * scale
        e = jnp.exp(s - jnp.max(s, axis=-1, keepdims=True))
        p = (e / jnp.sum(e, axis=-1, keepdims=True)).astype(BF16)
        outs.append(jnp.dot(p, v_ref[h], preferred_element_type=F32).astype(BF16))
    o = jnp.concatenate(outs, axis=-1)
    o_ref[...] = x + jnp.dot(o, wo_ref[...], preferred_element_type=F32)


def _xattn_fused(x2, g, w_q, kh, vh, layer, w_o, *, seq_len, tt):
    m, d = x2.shape
    _, _, heads, mlen, hd = kh.shape
    per_seq = seq_len // tt
    kv = pl.BlockSpec((None, None, heads, mlen, hd), lambda i: (layer, i // per_seq, 0, 0, 0))
    tile = pl.BlockSpec((tt, d), lambda i: (i, 0))
    return pl.pallas_call(
        functools.partial(_xattn_kernel, scale=hd ** -0.5),
        grid=(m // tt,),
        in_specs=[tile, _resident((1, d)), _resident(w_q.shape), kv, kv, _resident(w_o.shape)],
        out_specs=tile,
        out_shape=jax.ShapeDtypeStruct((m, d), F32),
        compiler_params=_params(1),
        name="xattn_fused",
    )(x2, g.reshape(1, d), w_q, kh, vh, w_o)


def _attn_kernel(q_ref, k_ref, v_ref, o_ref, *, scale):
    bb, tt, _ = q_ref.shape
    m, heads, hd = k_ref.shape[1], k_ref.shape[2], k_ref.shape[3]
    row_head = lax.broadcasted_iota(jnp.int32, (heads * tt, m * heads), 0) // tt
    col_head = lax.broadcasted_iota(jnp.int32, (heads * tt, m * heads), 1) % heads
    own = row_head == col_head
    for i in range(bb):
        k2 = k_ref[i].reshape(m * heads, hd).astype(BF16)
        v2 = v_ref[i].reshape(m * heads, hd).astype(BF16)
        q2 = jnp.concatenate([q_ref[i, :, h * hd:(h + 1) * hd] for h in range(heads)], axis=0).astype(BF16)
        s = lax.dot_general(q2, k2, NT, preferred_element_type=F32) * scale
        s = jnp.where(own, s, -jnp.inf)
        e = jnp.exp(s - jnp.max(s, axis=-1, keepdims=True))
        p = (e / jnp.sum(e, axis=-1, keepdims=True)).astype(BF16)
        o2 = jnp.dot(p, v2, preferred_element_type=F32)
        o_ref[i] = jnp.concatenate([o2[h * tt:(h + 1) * tt] for h in range(heads)], axis=-1).astype(o_ref.dtype)


def _attn(q, mem_k, mem_v, layer, *, bb, tt, out_dtype):
    b, t, d = q.shape
    _, _, m, heads, hd = mem_k.shape
    kv = pl.BlockSpec((None, bb, m, heads, hd), lambda i, j: (layer, i, 0, 0, 0))
    return pl.pallas_call(
        functools.partial(_attn_kernel, scale=hd ** -0.5),
        grid=(b // bb, t // tt),
        in_specs=[pl.BlockSpec((bb, tt, d), lambda i, j: (i, j, 0)), kv, kv],
        out_specs=pl.BlockSpec((bb, tt, d), lambda i, j: (i, j, 0)),
        out_shape=jax.ShapeDtypeStruct((b, t, d), out_dtype),
        compiler_params=_params(2),
        name="attn_core",
    )(q, mem_k, mem_v)


def _conv_kernel(x_ref, c0_ref, g_ref, win_ref, bin_ref, f_ref, bdw_ref, lg_ref, lb_ref, wout_ref, bout_ref,
                 o_ref, cs_ref, up_ref, *, lane_chunk):
    bb, tt, d = x_ref.shape
    t = pl.program_id(1)
    lo = CONV_PAD - CONV_STATE
    x = x_ref[...].reshape(bb * tt, d)
    pre = jnp.dot(_rms(x, g_ref[...]).astype(BF16), win_ref[...], preferred_element_type=F32) + bin_ref[...]
    u = (pre[:, :d] * jax.nn.sigmoid(pre[:, d:])).reshape(bb, tt, d)

    @pl.when(t == 0)
    def _():
        up_ref[:, 0:lo, :] = jnp.zeros((bb, lo, d), F32)
        up_ref[:, CONV_PAD + tt:, :] = jnp.zeros((bb, SUBLANES, d), F32)
        up_ref[:, lo:CONV_PAD, :] = c0_ref[...]

    up_ref[:, CONV_PAD:CONV_PAD + tt, :] = u
    acts = []
    for i in range(bb):
        zs = []
        for c in range(d // lane_chunk):
            cs = slice(c * lane_chunk, (c + 1) * lane_chunk)
            acc = None
            for o in range(SUBLANES):
                g = None
                for j in range(CONV_WIDTH):
                    if (lo + j) % SUBLANES != o:
                        continue
                    base = lo + j - o
                    term = up_ref[i, base:base + tt + SUBLANES, cs] * f_ref[j:j + 1, cs]
                    g = term if g is None else g + term
                shifted = g[o:o + tt]
                acc = shifted if acc is None else acc + shifted
            zs.append(acc)
        z = jnp.concatenate(zs, axis=-1) + bdw_ref[...]
        zc = z - jnp.mean(z, axis=-1, keepdims=True)
        zn = zc * lax.rsqrt(jnp.mean(zc * zc, axis=-1, keepdims=True) + LN_EPS)
        zn = zn * lg_ref[...] + lb_ref[...]
        acts.append((zn * jax.nn.sigmoid(zn)).astype(BF16))
    act = acts[0] if bb == 1 else jnp.concatenate(acts, axis=0)
    out = x + jnp.dot(act, wout_ref[...], preferred_element_type=F32) + bout_ref[...]
    o_ref[...] = out.reshape(bb, tt, d)
    tail = up_ref[:, tt + lo:tt + CONV_PAD, :]
    cs_ref[...] = tail
    up_ref[:, lo:CONV_PAD, :] = tail


def _conv_mixer(x, g, conv_all, ci, carry, p, *, bb, tt):
    b, t, d = x.shape
    vec = lambda a: a.reshape(1, -1)
    slab = pl.BlockSpec((None, bb, CONV_STATE, d), lambda i, j: (ci, i, 0, 0))
    args = [x, conv_all, vec(g), p['cv_w_in'][ci], vec(p['cv_b_in'][ci]), p['cv_w_dw'][ci], vec(p['cv_b_dw'][ci]),
            vec(p['cv_ln_g'][ci]), vec(p['cv_ln_b'][ci]), p['cv_w_out'][ci], vec(p['cv_b_out'][ci])]
    in_specs = [pl.BlockSpec((bb, tt, d), lambda i, j: (i, j, 0)), slab] + [_resident(a.shape) for a in args[2:]]
    body = functools.partial(_conv_kernel, lane_chunk=2 * LANES)
    aliases = {}
    if carry is not None:
        args.append(carry)
        in_specs.append(pl.BlockSpec(memory_space=pl.ANY))
        aliases = {len(args) - 1: 1}
        body = _drop_ref(body, len(args) - 1)
    return pl.pallas_call(
        body,
        grid=(b // bb, t // tt),
        in_specs=in_specs,
        out_specs=[pl.BlockSpec((bb, tt, d), lambda i, j: (i, j, 0)), slab],
        out_shape=[jax.ShapeDtypeStruct((b, t, d), F32), jax.ShapeDtypeStruct(conv_all.shape, F32)],
        scratch_shapes=[pltpu.VMEM((bb, CONV_PAD + tt + SUBLANES, d), F32)],
        input_output_aliases=aliases,
        compiler_params=_params(2),
        name="conv_core",
    )(*args)


def _rwkv_proj_kernel(*refs, has_vfirst):
    (x_ref, sh0_ref, g_ref, mix_ref, wr_ref, wk_ref, wv_ref, w0_ref, w1_ref, w2_ref,
     a0_ref, a1_ref, a2_ref, g1_ref, g2_ref, kk_ref, ka_ref, bd_ref) = refs[:18]
    pos = 18
    if has_vfirst:
        vf_ref, v0_ref, v1_ref, v2_ref = refs[pos:pos + 4]
        pos += 4
    (r_o, lw_o, k_o, v_o, kkn_o, b_o, g_o, sh_o) = refs[pos:pos + 8]
    hs_ref = refs[pos + 8]

    bb, tt, d = x_ref.shape
    t = pl.program_id(1)
    h3 = _rms(x_ref[...], g_ref[...])

    @pl.when(t == 0)
    def _():
        hs_ref[:, 7:8, :] = sh0_ref[...]

    hs_ref[:, 8:8 + tt, :] = h3
    prev3 = hs_ref[:, 7:7 + tt, :]
    last = h3[:, tt - 1:tt, :]
    hs_ref[:, 7:8, :] = last
    sh_o[...] = last

    m = bb * tt
    h = h3.reshape(m, d)
    xx = prev3.reshape(m, d) - h
    mix = mix_ref[...]
    xr, xw, xk, xv, xa, xg = ((h + xx * mix[i:i + 1, :]).astype(BF16) for i in range(6))

    r = jnp.dot(xr, wr_ref[...], preferred_element_type=F32)
    k = jnp.dot(xk, wk_ref[...], preferred_element_type=F32)
    v = jnp.dot(xv, wv_ref[...], preferred_element_type=F32)
    w_pre = w0_ref[...] + _bdot(jnp.tanh(jnp.dot(xw, w1_ref[...], preferred_element_type=F32)), w2_ref[...])
    lw = -math.exp(-0.5) * jax.nn.sigmoid(w_pre)
    if has_vfirst:
        gate = jax.nn.sigmoid(v0_ref[...] + _bdot(jnp.dot(xv, v1_ref[...], preferred_element_type=F32), v2_ref[...]))
        v = v + (vf_ref[...].reshape(m, d) - v) * gate
    a = jax.nn.sigmoid(a0_ref[...] + _bdot(jnp.dot(xa, a1_ref[...], preferred_element_type=F32), a2_ref[...]))
    g = _bdot(jax.nn.sigmoid(jnp.dot(xg, g1_ref[...], preferred_element_type=F32)), g2_ref[...])
    kk = k * kk_ref[...]
    nrm = jnp.maximum(jnp.sqrt(_segsum(kk * kk, bd_ref[...], 3)), L2_EPS)
    kk = kk / nrm
    k = k * (1.0 + (a - 1.0) * ka_ref[...])

    shp = (bb, tt, d)
    r_o[...] = r.reshape(shp)
    lw_o[...] = lw.reshape(shp)
    k_o[...] = k.reshape(shp)
    v_o[...] = v.reshape(shp)
    kkn_o[...] = kk.reshape(shp)
    b_o[...] = (kk * a).reshape(shp)
    g_o[...] = g.reshape(shp)


def _rwkv_proj(x, shift0, g, p, ri, v_first, bd, *, bb, tt):
    b, t, d = x.shape
    vec = lambda a: a.reshape(1, d)
    tile = pl.BlockSpec((bb, tt, d), lambda i, j: (i, j, 0))
    row = pl.BlockSpec((bb, 1, d), lambda i, j: (i, 0, 0))
    args = [x, shift0.reshape(b, 1, d), vec(g), p['rw_mix'][ri],
            p['rw_w_r'][ri], p['rw_w_k'][ri], p['rw_w_v'][ri],
            vec(p['rw_w0'][ri]), p['rw_w1'][ri], p['rw_w2'][ri],
            vec(p['rw_a0'][ri]), p['rw_a1'][ri], p['rw_a2'][ri],
            p['rw_g1'][ri], p['rw_g2'][ri], vec(p['rw_k_k'][ri]), vec(p['rw_k_a'][ri]), bd]
    in_specs = [tile, row] + [_resident(a.shape) for a in args[2:]]
    has_vfirst = v_first is not None
    if has_vfirst:
        vi = ri - 1
        extra = [v_first, vec(p['rw_v0'][vi]), p['rw_v1'][vi], p['rw_v2'][vi]]
        args += extra
        in_specs += [tile] + [_resident(a.shape) for a in extra[1:]]
    big = jax.ShapeDtypeStruct((b, t, d), F32)
    outs = pl.pallas_call(
        functools.partial(_rwkv_proj_kernel, has_vfirst=has_vfirst),
        grid=(b // bb, t // tt),
        in_specs=in_specs,
        out_specs=[tile] * 7 + [row],
        out_shape=[big] * 7 + [jax.ShapeDtypeStruct((b, 1, d), F32)],
        scratch_shapes=[pltpu.VMEM((bb, tt + 8, d), F32)],
        compiler_params=_params(2),
        name="rwkv_proj",
    )(*args)
    return outs


def _scan_kernel(r_ref, lw_ref, k_ref, v_ref, kk_ref, b_ref, s0_ref, tri_ref, y_ref, so_ref, s_scr, *, tc, C):
    bb, _, d = r_ref.shape
    n = RWKV_HEAD
    n_pairs = d // LANES
    t = pl.program_id(1)

    @pl.when(t == 0)
    def _():
        for i in range(bb):
            for p in range(n_pairs):
                s_scr[i, p] = jnp.concatenate([s0_ref[i, 2 * p], s0_ref[i, 2 * p + 1]], axis=-1)

    def load(ref, i):
        x = ref[i]
        if tc < C:
            x = jnp.concatenate([x, jnp.zeros((C - tc, d), F32)], axis=0)
        return x

    lane_lo = lax.broadcasted_iota(jnp.int32, (1, LANES), 1) < n
    row = lax.broadcasted_iota(jnp.int32, (2 * C, 2 * C), 0)
    col = lax.broadcasted_iota(jnp.int32, (2 * C, 2 * C), 1)
    row_hi = row >= C
    col_hi = col >= C
    tr = jnp.where(row_hi, row - C, row)
    tj = jnp.where(col_hi, col - C, col)
    diag = row_hi == col_hi
    m_l = jnp.logical_and(diag, tj < tr)
    m_ak = jnp.logical_and(jnp.logical_not(diag), tj < tr)
    m_incl = tj <= tr
    eye = (row == col).astype(F32)
    rl_row = lax.broadcasted_iota(jnp.int32, (2 * C, LANES), 0) >= C
    rl_lane = lax.broadcasted_iota(jnp.int32, (2 * C, LANES), 1) >= n
    m_rl = rl_row == rl_lane

    def stack_masked(x):
        return jnp.concatenate([jnp.where(lane_lo, x, 0.0), jnp.where(lane_lo, 0.0, x)], axis=0).astype(BF16)

    units = []
    for i in range(bb):
        r, lw, k, v, kk, b = (load(ref, i) for ref in (r_ref, lw_ref, k_ref, v_ref, kk_ref, b_ref))
        cl = _split_dot_lhs(tri_ref[...], lw, 3)
        cl_end = cl[C - 1:C, :]
        e_neg = jnp.exp(-cl)
        e_end = jnp.exp(cl_end - cl)
        a_hat = -kk * jnp.exp(cl - lw)
        r_hat = r * jnp.exp(cl)
        b_hat = b * e_neg
        k_hat = k * e_neg
        b_bar = b * e_end
        k_bar = k * e_end
        w_end = jnp.exp(cl_end)
        for p in range(n_pairs):
            sl = slice(p * LANES, (p + 1) * LANES)
            units.append(dict(i=i, p=p, sl=sl, vv=v[:, sl], a_hat=a_hat[:, sl], r_hat=r_hat[:, sl],
                              b_hat=b_hat[:, sl], k_hat=k_hat[:, sl], b_bar=b_bar[:, sl], k_bar=k_bar[:, sl],
                              w_end=w_end[:, sl]))

    for u in units:
        bh, kh = u['b_hat'], u['k_hat']
        u['s_old'] = s_scr[u['i'], u['p']]
        rhs = jnp.concatenate([u['s_old'], u['s_old'], jnp.where(lane_lo, bh, kh), jnp.where(lane_lo, kh, bh)],
                              axis=0).astype(BF16)
        lhs = jnp.concatenate([stack_masked(u['a_hat']), stack_masked(u['r_hat'])], axis=0)
        gp = lax.dot_general(lhs, rhs, NT, preferred_element_type=F32)
        u['p_a'], g_a = gp[:2 * C, :LANES], gp[:2 * C, LANES:]
        u['p_r'], g_r = gp[2 * C:, :LANES], gp[2 * C:, LANES:]
        u['l_mat'] = jnp.where(m_l, g_a, 0.0)
        u['t_mat'] = eye + u['l_mat']
        u['ak'] = jnp.where(m_ak, g_a, 0.0)
        u['g_r'] = jnp.where(m_incl, g_r, 0.0)

    n_factors = int(math.log2(C)) - 1
    for u in units:
        u['l_pow'] = _bdot(u['l_mat'], u['l_mat'])
    for _ in range(n_factors - 1):
        for u in units:
            both = _bdot(jnp.concatenate([u['t_mat'], u['l_pow']], axis=0), u['l_pow'])
            u['t_mat'] = u['t_mat'] + both[:2 * C]
            u['l_pow'] = both[2 * C:]
    for u in units:
        u['t_mat'] = u['t_mat'] + _bdot(u['t_mat'], u['l_pow'])

    for u in units:
        vv2 = jnp.concatenate([u['vv'], u['vv']], axis=0)
        u['rhs_u'] = jnp.where(m_rl, u['p_a'] + _bdot(u['ak'], vv2), 0.0)
    for u in units:
        u_st = _bdot(u['t_mat'], u['rhs_u'])
        u['u_lo'], u['u_hi'] = u_st[:C], u_st[C:]
    for u in units:
        vv = u['vv']
        w_mix = jnp.concatenate([jnp.where(lane_lo, u['u_lo'], vv), jnp.where(lane_lo, vv, u['u_hi'])], axis=0)
        y_st = jnp.where(m_rl, u['p_r'] + _bdot(u['g_r'], w_mix), 0.0)
        y = y_st[:C] + y_st[C:]
        y_ref[u['i'], :, u['sl']] = y[:tc]
    for u in units:
        uv_t = jnp.concatenate([u['u_lo'] + u['u_hi'], u['vv']], axis=0).T
        bk = jnp.concatenate([u['b_bar'], u['k_bar']], axis=0)
        full = _bdot(uv_t, bk)
        s_new = u['s_old'] * u['w_end'] + jnp.where(lane_lo, full[:n], full[n:])
        s_scr[u['i'], u['p']] = s_new

    @pl.when(t == pl.num_programs(1) - 1)
    def _():
        for i in range(bb):
            for p in range(n_pairs):
                s_fin = s_scr[i, p]
                so_ref[i, 2 * p] = s_fin[:, :n]
                so_ref[i, 2 * p + 1] = s_fin[:, n:]


def _rwkv_scan(r, lw, k, v, kk, b, s_all, ri, carry, tri, *, bb, tc):
    bsz, t, d = r.shape
    n_pairs = d // LANES
    chunk = tri.shape[0]
    tile = pl.BlockSpec((bb, tc, d), lambda i, j: (i, j, 0))
    st = pl.BlockSpec((None, bb) + s_all.shape[2:], lambda i, j: (ri, i, 0, 0, 0))
    args = [r, lw, k, v, kk, b, s_all, tri]
    in_specs = [tile] * 6 + [st, _resident(tri.shape)]
    body = functools.partial(_scan_kernel, tc=tc, C=chunk)
    aliases = {}
    if carry is not None:
        args.append(carry)
        in_specs.append(pl.BlockSpec(memory_space=pl.ANY))
        aliases = {len(args) - 1: 1}
        body = _drop_ref(body, len(args) - 1)
    return pl.pallas_call(
        body,
        grid=(bsz // bb, t // tc),
        in_specs=in_specs,
        out_specs=[tile, st],
        out_shape=[jax.ShapeDtypeStruct((bsz, t, d), F32), jax.ShapeDtypeStruct(s_all.shape, F32)],
        scratch_shapes=[pltpu.VMEM((bb, n_pairs, RWKV_HEAD, LANES), F32)],
        input_output_aliases=aliases,
        compiler_params=_params(2),
        name="rwkv_scan",
    )(*args)


def _rwkv_out_kernel(y_ref, r_ref, k_ref, v_ref, g_ref, x_ref, lg_ref, lb_ref, rk_ref, wo_ref, bd_ref, o_ref):
    bd = bd_ref[...]
    inv_n = 1.0 / RWKV_HEAD
    y = y_ref[...]
    yc = y - _segsum(y, bd, 2) * inv_n
    var = _segsum(yc * yc, bd, 2) * inv_n
    yn = yc * lax.rsqrt(var + LNX_EPS) * lg_ref[...] + lb_ref[...]
    v = v_ref[...]
    bonus = _segsum(r_ref[...] * k_ref[...] * rk_ref[...], bd, 2) * v
    z = ((yn + bonus) * g_ref[...]).astype(BF16)
    o_ref[...] = x_ref[...] + jnp.dot(z, wo_ref[...], preferred_element_type=F32)


def _rwkv_out(y, r, k, v, g, x, lnx_g, lnx_b, r_k, w_o, bd, *, tm=256):
    m, d = x.shape
    tm = min(tm, m)
    tile = pl.BlockSpec((tm, d), lambda i: (i, 0))
    vec = lambda a: a.reshape(1, d)
    return pl.pallas_call(
        _rwkv_out_kernel,
        grid=(m // tm,),
        in_specs=[tile] * 6 + [_resident((1, d))] * 3 + [_resident(w_o.shape), _resident(bd.shape)],
        out_specs=tile,
        out_shape=jax.ShapeDtypeStruct((m, d), F32),
        compiler_params=_params(1),
        name="rwkv_out",
    )(y, r, k, v, g, x, vec(lnx_g), vec(lnx_b), vec(r_k), w_o, bd)


def _trunk(x, mem_k, mem_v, conv_state, shift_state, wkv_state, p, consts, cfg, mem_heads=None):
    bsz, t, d = x.shape
    m = bsz * t
    depth = p['norm_g'].shape[0]
    bd = consts
    ar = jnp.arange(cfg['scan_chunk'])
    tri = (ar[None, :] <= ar[:, None]).astype(BF16)
    x2 = x.reshape(m, d)
    conv_new, wkv_new, shift_new = None, None, []
    v_first = None
    for layer in range(depth):
        g = p['norm_g'][layer]
        if layer % 2 == 0:
            ci = layer // 2
            x3, conv_new = _conv_mixer(x2.reshape(bsz, t, d), g[0], conv_state, ci, conv_new, p,
                                       bb=cfg['conv_bb'], tt=cfg['conv_tt'])
            x2 = x3.reshape(m, d)
        else:
            ri = layer // 2
            r, lw, k, v, kk, b, gt, sh = _rwkv_proj(x2.reshape(bsz, t, d), shift_state[ri], g[0], p, ri, v_first, bd,
                                                    bb=cfg['proj_bb'], tt=cfg['proj_tt'])
            if v_first is None:
                v_first = v
            y, wkv_new = _rwkv_scan(r, lw, k, v, kk, b, wkv_state, ri, wkv_new, tri,
                                    bb=cfg['scan_bb'], tc=cfg['scan_tc'])
            shift_new.append(sh.reshape(bsz, d))
            flat = lambda a: a.reshape(m, d)
            x2 = _rwkv_out(flat(y), flat(r), flat(k), flat(v), flat(gt), x2, p['rw_lnx_g'][ri], p['rw_lnx_b'][ri],
                           p['rw_r_k'][ri].reshape(d), p['rw_w_o'][ri], bd)
        if mem_heads is not None:
            x2 = _xattn_fused(x2, g[1], p['xa_w_q'][layer], mem_heads[0], mem_heads[1], layer, p['xa_w_o'][layer],
                              seq_len=t, tt=cfg['attn_tt'])
        else:
            q = _norm_matmul(x2, g[1], p['xa_w_q'][layer], out_dtype=cfg['attn_dtype'])
            o = _attn(q.reshape(bsz, t, d), mem_k, mem_v, layer, bb=cfg['attn_bb'], tt=cfg['attn_tt'],
                      out_dtype=cfg['attn_dtype'])
            x2 = _matmul_res(o.reshape(m, d), p['xa_w_o'][layer], x2)
        x2 = _mlp(x2, g[2], p['mlp_w_up'][layer], p['mlp_w_down'][layer], p['final_g'],
                  final_norm=layer == depth - 1)
    return x2.reshape(bsz, t, d), conv_new, jnp.stack(shift_new), wkv_new


PROMPT_CFG = dict(conv_bb=1, conv_tt=128, proj_bb=1, proj_tt=256, scan_bb=2, scan_tc=SCAN_CHUNK,
                  scan_chunk=SCAN_CHUNK, attn_bb=1, attn_tt=512, attn_dtype=BF16)
SAMPLE_CFG = dict(conv_bb=16, conv_tt=8, proj_bb=16, proj_tt=8, scan_bb=4, scan_tc=8,
                  scan_chunk=8, attn_bb=4, attn_tt=8, attn_dtype=F32)


def kernel(x_prompt, x_sample, mem_prompt, cache_mem_k, cache_mem_v, state_conv, state_shift, state_wkv, norm_g, final_g, cv_w_in, cv_b_in, cv_w_dw, cv_b_dw, cv_ln_g, cv_ln_b, cv_w_out, cv_b_out, rw_mix, rw_w_r, rw_w_k, rw_w_v, rw_w_o, rw_w0, rw_w1, rw_w2, rw_a0, rw_a1, rw_a2, rw_v0, rw_v1, rw_v2, rw_g1, rw_g2, rw_k_k, rw_k_a, rw_r_k, rw_lnx_g, rw_lnx_b, xa_w_q, xa_w_kv, xa_w_o, mlp_w_up, mlp_w_down):
    bf = lambda w: w.astype(BF16)
    p = {
        'norm_g': norm_g, 'final_g': final_g,
        'cv_w_in': bf(cv_w_in), 'cv_b_in': cv_b_in, 'cv_w_dw': cv_w_dw, 'cv_b_dw': cv_b_dw,
        'cv_ln_g': cv_ln_g, 'cv_ln_b': cv_ln_b, 'cv_w_out': bf(cv_w_out), 'cv_b_out': cv_b_out,
        'rw_mix': rw_mix, 'rw_w_r': bf(rw_w_r), 'rw_w_k': bf(rw_w_k), 'rw_w_v': bf(rw_w_v), 'rw_w_o': bf(rw_w_o),
        'rw_w0': rw_w0, 'rw_w1': bf(rw_w1), 'rw_w2': bf(rw_w2), 'rw_a0': rw_a0, 'rw_a1': bf(rw_a1),
        'rw_a2': bf(rw_a2), 'rw_v0': rw_v0, 'rw_v1': bf(rw_v1), 'rw_v2': bf(rw_v2), 'rw_g1': bf(rw_g1),
        'rw_g2': bf(rw_g2), 'rw_k_k': rw_k_k, 'rw_k_a': rw_k_a, 'rw_r_k': rw_r_k,
        'rw_lnx_g': rw_lnx_g, 'rw_lnx_b': rw_lnx_b,
        'xa_w_q': bf(xa_w_q), 'xa_w_o': bf(xa_w_o), 'mlp_w_up': bf(mlp_w_up), 'mlp_w_down': bf(mlp_w_down),
    }
    depth = norm_g.shape[0]
    bp, mlen, d = mem_prompt.shape
    n_conv, n_rwkv = state_conv.shape[0], state_wkv.shape[0]
    heads, n = state_wkv.shape[2], state_wkv.shape[3]
    hd = d // XA_HEADS

    idx = jnp.arange(SEG_TILE) // RWKV_HEAD
    bd = (idx[:, None] == idx[None, :]).astype(BF16)
    consts = bd

    mem_k_prompt, mem_v_prompt, mem_kh, mem_vh = _memkv(mem_prompt, bf(xa_w_kv))

    conv0 = jnp.zeros((n_conv, bp, CONV_STATE, d), F32)
    shift0 = jnp.zeros((n_rwkv, bp, d), F32)
    wkv0 = jnp.zeros((n_rwkv, bp, heads, n, n), F32)
    y_prompt, conv_prompt, shift_prompt, wkv_prompt = _trunk(
        x_prompt, mem_k_prompt, mem_v_prompt, conv0, shift0, wkv0, p, consts, PROMPT_CFG,
        mem_heads=(mem_kh, mem_vh))
    y_sample, conv_sample, shift_sample, wkv_sample = _trunk(
        x_sample, cache_mem_k, cache_mem_v, state_conv, state_shift, state_wkv, p, consts, SAMPLE_CFG)
    return (y_prompt, y_sample, mem_k_prompt, mem_v_prompt, conv_prompt, shift_prompt, wkv_prompt,
            conv_sample, shift_sample, wkv_sample)
```

```python
import functools
import math

import jax
import jax.numpy as jnp
from jax import lax
from jax.experimental import pallas as pl
from jax.experimental.pallas import tpu as pltpu

F32 = jnp.float32
BF16 = jnp.bfloat16

RMS_EPS = 1e-6
LN_EPS = 1e-5
L2_EPS = 1e-12
LNX_EPS = 64e-5
CONV_WIDTH = 31
CONV_STATE = CONV_WIDTH - 1
CONV_PAD = 32
RWKV_HEAD = 64
XA_HEADS = 4
SCAN_CHUNK = 64
LANES = 128
SUBLANES = 8
SEG_TILE = 256
VMEM_LIMIT = 56 * 1024 * 1024

NT = (((1,), (1,)), ((), ()))


def _params(n_axes, sem=None):
    if sem is None:
        sem = ("arbitrary",) * n_axes
    return pltpu.CompilerParams(dimension_semantics=sem, vmem_limit_bytes=VMEM_LIMIT)


def _resident(shape):
    nd = len(shape)
    return pl.BlockSpec(tuple(shape), lambda *_: (0,) * nd, pipeline_mode=pl.Buffered(1))


def _drop_ref(body, pos):
    def wrapped(*refs):
        return body(*refs[:pos], *refs[pos + 1:])
    return wrapped


def _rms(x, g):
    ms = jnp.mean(x * x, axis=-1, keepdims=True)
    return x * lax.rsqrt(ms + RMS_EPS) * g


def _bdot(a, b):
    return jnp.dot(a.astype(BF16), b.astype(BF16), preferred_element_type=F32)


def _split_dot_rhs(x, w, passes):
    acc = None
    rem = x
    for i in range(passes):
        part = rem.astype(BF16)
        term = jnp.dot(part, w, preferred_element_type=F32)
        acc = term if acc is None else acc + term
        if i + 1 < passes:
            rem = rem - part.astype(F32)
    return acc


def _split_dot_lhs(w, x, passes):
    acc = None
    rem = x
    for i in range(passes):
        part = rem.astype(BF16)
        term = jnp.dot(w, part, preferred_element_type=F32)
        acc = term if acc is None else acc + term
        if i + 1 < passes:
            rem = rem - part.astype(F32)
    return acc


def _segsum(x, bd, passes):
    d = x.shape[-1]
    outs = []
    for c in range(d // SEG_TILE):
        outs.append(_split_dot_rhs(x[:, c * SEG_TILE:(c + 1) * SEG_TILE], bd, passes))
    return jnp.concatenate(outs, axis=-1)


def _norm_matmul_kernel(*refs, glu, has_bias):
    if has_bias:
        x_ref, g_ref, w_ref, b_ref, o_ref = refs
    else:
        x_ref, g_ref, w_ref, o_ref = refs
    h = _rms(x_ref[...], g_ref[...]).astype(BF16)
    r = jnp.dot(h, w_ref[...], preferred_element_type=F32)
    if has_bias:
        r = r + b_ref[...]
    if glu:
        d = r.shape[-1] // 2
        r = r[:, :d] * jax.nn.sigmoid(r[:, d:])
    o_ref[...] = r.astype(o_ref.dtype)


def _norm_matmul(x, g, w, b=None, *, glu=False, out_dtype=F32, tm=512):
    m, d = x.shape
    n = w.shape[1]
    n_out = n // 2 if glu else n
    tm = min(tm, m)
    in_specs = [pl.BlockSpec((tm, d), lambda i: (i, 0)), _resident((1, d)), _resident((d, n))]
    args = [x, g.reshape(1, d), w]
    if b is not None:
        in_specs.append(_resident((1, n)))
        args.append(b.reshape(1, n))
    return pl.pallas_call(
        functools.partial(_norm_matmul_kernel, glu=glu, has_bias=b is not None),
        grid=(m // tm,),
        in_specs=in_specs,
        out_specs=pl.BlockSpec((tm, n_out), lambda i: (i, 0)),
        out_shape=jax.ShapeDtypeStruct((m, n_out), out_dtype),
        compiler_params=_params(1),
        name="norm_matmul",
    )(*args)


def _matmul_res_kernel(*refs, has_bias):
    if has_bias:
        a_ref, w_ref, b_ref, x_ref, o_ref = refs
    else:
        a_ref, w_ref, x_ref, o_ref = refs
    r = jnp.dot(a_ref[...].astype(BF16), w_ref[...], preferred_element_type=F32)
    if has_bias:
        r = r + b_ref[...]
    o_ref[...] = x_ref[...] + r


def _matmul_res(a, w, x, b=None, *, tm=512):
    m, k = a.shape
    d = w.shape[1]
    tm = min(tm, m)
    in_specs = [pl.BlockSpec((tm, k), lambda i: (i, 0)), _resident((k, d))]
    args = [a, w]
    if b is not None:
        in_specs.append(_resident((1, d)))
        args.append(b.reshape(1, d))
    in_specs.append(pl.BlockSpec((tm, d), lambda i: (i, 0)))
    args.append(x)
    return pl.pallas_call(
        functools.partial(_matmul_res_kernel, has_bias=b is not None),
        grid=(m // tm,),
        in_specs=in_specs,
        out_specs=pl.BlockSpec((tm, d), lambda i: (i, 0)),
        out_shape=jax.ShapeDtypeStruct((m, d), F32),
        compiler_params=_params(1),
        name="matmul_res",
    )(*args)


MLP_CHUNKS = 4


def _mlp_block(x, g_ref, wu_ref, wd_ref, fg_ref, final_norm):
    h = _rms(x, g_ref[...]).astype(BF16)
    ck = wu_ref.shape[1] // MLP_CHUNKS
    acc = x
    for c in range(MLP_CHUNKS):
        u = jnp.dot(h, wu_ref[:, c * ck:(c + 1) * ck], preferred_element_type=F32)
        u = jnp.maximum(u, 0.0)
        u = (u * u).astype(BF16)
        acc = acc + jnp.dot(u, wd_ref[c * ck:(c + 1) * ck, :], preferred_element_type=F32)
    if final_norm:
        acc = _rms(acc, fg_ref[...])
    return acc


def _mlp_kernel(x_ref, g_ref, wu_ref, wd_ref, fg_ref, o_ref, *, final_norm):
    o_ref[...] = _mlp_block(x_ref[...], g_ref, wu_ref, wd_ref, fg_ref, final_norm)


def _mlp(x, g, w_up, w_down, final_g, *, final_norm, tm=512):
    m, d = x.shape
    dff = w_up.shape[1]
    tm = min(tm, m)
    return pl.pallas_call(
        functools.partial(_mlp_kernel, final_norm=final_norm),
        grid=(m // tm,),
        in_specs=[pl.BlockSpec((tm, d), lambda i: (i, 0)), _resident((1, d)),
                  _resident((d, dff)), _resident((dff, d)), _resident((1, d))],
        out_specs=pl.BlockSpec((tm, d), lambda i: (i, 0)),
        out_shape=jax.ShapeDtypeStruct((m, d), F32),
        compiler_params=_params(1),
        name="mlp",
    )(x, g.reshape(1, d), w_up, w_down, final_g.reshape(1, d))


def _memkv_kernel(x_ref, w_ref, k_ref, v_ref, kh_ref, vh_ref):
    r = jnp.dot(x_ref[...].astype(BF16), w_ref[0], preferred_element_type=F32)
    tb, mlen, heads, hd = k_ref.shape
    d = heads * hd
    for s in range(tb):
        rows = slice(s * mlen, (s + 1) * mlen)
        for h in range(heads):
            kb = r[rows, h * hd:(h + 1) * hd]
            vb = r[rows, d + h * hd:d + (h + 1) * hd]
            k_ref[s, :, h, :] = kb
            v_ref[s, :, h, :] = vb
            kh_ref[s, h] = kb.astype(BF16)
            vh_ref[s, h] = vb.astype(BF16)


def _memkv(mem, w_kv, *, tb=2):
    bp, mlen, d = mem.shape
    depth = w_kv.shape[0]
    hd = d // XA_HEADS
    out = jax.ShapeDtypeStruct((depth, bp, mlen, XA_HEADS, hd), F32)
    out_h = jax.ShapeDtypeStruct((depth, bp, XA_HEADS, mlen, hd), BF16)
    blk = pl.BlockSpec((None, tb, mlen, XA_HEADS, hd), lambda l, i: (l, i, 0, 0, 0))
    blk_h = pl.BlockSpec((None, tb, XA_HEADS, mlen, hd), lambda l, i: (l, i, 0, 0, 0))
    return pl.pallas_call(
        _memkv_kernel,
        grid=(depth, bp // tb),
        in_specs=[pl.BlockSpec((tb * mlen, d), lambda l, i: (i, 0)),
                  pl.BlockSpec((1, d, 2 * d), lambda l, i: (l, 0, 0))],
        out_specs=[blk, blk, blk_h, blk_h],
        out_shape=[out, out, out_h, out_h],
        compiler_params=_params(2),
        name="memkv",
    )(mem.reshape(bp * mlen, d), w_kv)


def _xattn_mlp_kernel(x_ref, ga_ref, wq_ref, k_ref, v_ref, wo_ref, gm_ref, wu_ref, wd_ref, fg_ref, o_ref,
                      *, scale, final_norm):
    heads, _, hd = k_ref.shape
    x = x_ref[...]
    q = jnp.dot(_rms(x, ga_ref[...]).astype(BF16), wq_ref[...], preferred_element_type=F32).astype(BF16)
    outs = []
    for h in range(heads):
        s = lax.dot_general(q[:, h * hd:(h + 1) * hd], k_ref[h], NT, preferred_element_type=F32) * scale
        e = jnp.exp(s - jnp.max(s, axis=-1, keepdims=True))
        p = (e / jnp.sum(e, axis=-1, keepdims=True)).astype(BF16)
        outs.append(jnp.dot(p, v_ref[h], preferred_element_type=F32).astype(BF16))
    o = jnp.concatenate(outs, axis=-1)
    x = x + jnp.dot(o, wo_ref[...], preferred_element_type=F32)
    o_ref[...] = _mlp_block(x, gm_ref, wu_ref, wd_ref, fg_ref, final_norm)


def _xattn_mlp_fused(x2, g_attn, w_q, kh, vh, layer, w_o, g_mlp, w_up, w_down, final_g, *, seq_len, tt,
                     final_norm):
    m, d = x2.shape
    _, _, heads, mlen, hd = kh.shape
    per_seq = seq_len // tt
    kv = pl.BlockSpec((None, None, heads, mlen, hd), lambda i: (layer, i // per_seq, 0, 0, 0))
    tile = pl.BlockSpec((tt, d), lambda i: (i, 0))
    vec = lambda a: a.reshape(1, d)
    return pl.pallas_call(
        functools.partial(_xattn_mlp_kernel, scale=hd ** -0.5, final_norm=final_norm),
        grid=(m // tt,),
        in_specs=[tile, _resident((1, d)), _resident(w_q.shape), kv, kv, _resident(w_o.shape),
                  _resident((1, d)), _resident(w_up.shape), _resident(w_down.shape), _resident((1, d))],
        out_specs=tile,
        out_shape=jax.ShapeDtypeStruct((m, d), F32),
        compiler_params=_params(1),
        name="xattn_mlp_fused",
    )(x2, vec(g_attn), w_q, kh, vh, w_o, vec(g_mlp), w_up, w_down, vec(final_g))


def _attn_kernel(q_ref, k_ref, v_ref, o_ref, *, scale):
    bb, tt, _ = q_ref.shape
    m, heads, hd = k_ref.shape[1], k_ref.shape[2], k_ref.shape[3]
    row_head = lax.broadcasted_iota(jnp.int32, (heads * tt, m * heads), 0) // tt
    col_head = lax.broadcasted_iota(jnp.int32, (heads * tt, m * heads), 1) % heads
    own = row_head == col_head
    for i in range(bb):
        k2 = k_ref[i].reshape(m * heads, hd).astype(BF16)
        v2 = v_ref[i].reshape(m * heads, hd).astype(BF16)
        q2 = jnp.concatenate([q_ref[i, :, h * hd:(h + 1) * hd] for h in range(heads)], axis=0).astype(BF16)
        s = lax.dot_general(q2, k2, NT, preferred_element_type=F32) * scale
        s = jnp.where(own, s, -jnp.inf)
        e = jnp.exp(s - jnp.max(s, axis=-1, keepdims=True))
        p = (e / jnp.sum(e, axis=-1, keepdims=True)).astype(BF16)
        o2 = jnp.dot(p, v2, preferred_element_type=F32)
        o_ref[i] = jnp.concatenate([o2[h * tt:(h + 1) * tt] for h in range(heads)], axis=-1).astype(o_ref.dtype)


def _attn(q, mem_k, mem_v, layer, *, bb, tt, out_dtype):
    b, t, d = q.shape
    _, _, m, heads, hd = mem_k.shape
    kv = pl.BlockSpec((None, bb, m, heads, hd), lambda i, j: (layer, i, 0, 0, 0))
    return pl.pallas_call(
        functools.partial(_attn_kernel, scale=hd ** -0.5),
        grid=(b // bb, t // tt),
        in_specs=[pl.BlockSpec((bb, tt, d), lambda i, j: (i, j, 0)), kv, kv],
        out_specs=pl.BlockSpec((bb, tt, d), lambda i, j: (i, j, 0)),
        out_shape=jax.ShapeDtypeStruct((b, t, d), out_dtype),
        compiler_params=_params(2),
        name="attn_core",
    )(q, mem_k, mem_v)


def _conv_kernel(x_ref, c0_ref, g_ref, win_ref, bin_ref, f_ref, bdw_ref, lg_ref, lb_ref, wout_ref, bout_ref,
                 o_ref, cs_ref, up_ref, *, lane_chunk):
    bb, tt, d = x_ref.shape
    t = pl.program_id(1)
    lo = CONV_PAD - CONV_STATE
    x = x_ref[...].reshape(bb * tt, d)
    pre = jnp.dot(_rms(x, g_ref[...]).astype(BF16), win_ref[...], preferred_element_type=F32) + bin_ref[...]
    u = (pre[:, :d] * jax.nn.sigmoid(pre[:, d:])).reshape(bb, tt, d)

    @pl.when(t == 0)
    def _():
        up_ref[:, 0:lo, :] = jnp.zeros((bb, lo, d), F32)
        up_ref[:, CONV_PAD + tt:, :] = jnp.zeros((bb, SUBLANES, d), F32)
        up_ref[:, lo:CONV_PAD, :] = c0_ref[...]

    up_ref[:, CONV_PAD:CONV_PAD + tt, :] = u
    acts = []
    for i in range(bb):
        zs = []
        for c in range(d // lane_chunk):
            cs = slice(c * lane_chunk, (c + 1) * lane_chunk)
            acc = None
            for o in range(SUBLANES):
                g = None
                for j in range(CONV_WIDTH):
                    if (lo + j) % SUBLANES != o:
                        continue
                    base = lo + j - o
                    term = up_ref[i, base:base + tt + SUBLANES, cs] * f_ref[j:j + 1, cs]
                    g = term if g is None else g + term
                shifted = g[o:o + tt]
                acc = shifted if acc is None else acc + shifted
            zs.append(acc)
        z = jnp.concatenate(zs, axis=-1) + bdw_ref[...]
        zc = z - jnp.mean(z, axis=-1, keepdims=True)
        zn = zc * lax.rsqrt(jnp.mean(zc * zc, axis=-1, keepdims=True) + LN_EPS)
        zn = zn * lg_ref[...] + lb_ref[...]
        acts.append((zn * jax.nn.sigmoid(zn)).astype(BF16))
    act = acts[0] if bb == 1 else jnp.concatenate(acts, axis=0)
    out = x + jnp.dot(act, wout_ref[...], preferred_element_type=F32) + bout_ref[...]
    o_ref[...] = out.reshape(bb, tt, d)
    tail = up_ref[:, tt + lo:tt + CONV_PAD, :]
    cs_ref[...] = tail
    up_ref[:, lo:CONV_PAD, :] = tail


def _conv_mixer(x, g, conv_all, ci, carry, p, *, bb, tt):
    b, t, d = x.shape
    vec = lambda a: a.reshape(1, -1)
    slab = pl.BlockSpec((None, bb, CONV_STATE, d), lambda i, j: (ci, i, 0, 0))
    args = [x, conv_all, vec(g), p['cv_w_in'][ci], vec(p['cv_b_in'][ci]), p['cv_w_dw'][ci], vec(p['cv_b_dw'][ci]),
            vec(p['cv_ln_g'][ci]), vec(p['cv_ln_b'][ci]), p['cv_w_out'][ci], vec(p['cv_b_out'][ci])]
    in_specs = [pl.BlockSpec((bb, tt, d), lambda i, j: (i, j, 0)), slab] + [_resident(a.shape) for a in args[2:]]
    body = functools.partial(_conv_kernel, lane_chunk=2 * LANES)
    aliases = {}
    if carry is not None:
        args.append(carry)
        in_specs.append(pl.BlockSpec(memory_space=pl.ANY))
        aliases = {len(args) - 1: 1}
        body = _drop_ref(body, len(args) - 1)
    return pl.pallas_call(
        body,
        grid=(b // bb, t // tt),
        in_specs=in_specs,
        out_specs=[pl.BlockSpec((bb, tt, d), lambda i, j: (i, j, 0)), slab],
        out_shape=[jax.ShapeDtypeStruct((b, t, d), F32), jax.ShapeDtypeStruct(conv_all.shape, F32)],
        scratch_shapes=[pltpu.VMEM((bb, CONV_PAD + tt + SUBLANES, d), F32)],
        input_output_aliases=aliases,
        compiler_params=_params(2),
        name="conv_core",
    )(*args)


def _rwkv_proj_kernel(*refs, has_vfirst):
    (x_ref, sh0_ref, g_ref, mix_ref, wr_ref, wk_ref, wv_ref, w0_ref, w1_ref, w2_ref,
     a0_ref, a1_ref, a2_ref, g1_ref, g2_ref, kk_ref, ka_ref, bd_ref) = refs[:18]
    pos = 18
    if has_vfirst:
        vf_ref, v0_ref, v1_ref, v2_ref = refs[pos:pos + 4]
        pos += 4
    (r_o, lw_o, k_o, v_o, kkn_o, b_o, g_o, sh_o) = refs[pos:pos + 8]
    hs_ref = refs[pos + 8]

    bb, tt, d = x_ref.shape
    t = pl.program_id(1)
    h3 = _rms(x_ref[...], g_ref[...])

    @pl.when(t == 0)
    def _():
        hs_ref[...] = sh0_ref[...]

    m = bb * tt
    h = h3.reshape(m, d)
    first = jnp.broadcast_to(hs_ref[...], (bb, tt, d)).reshape(m, d)
    is_first = lax.broadcasted_iota(jnp.int32, (m, d), 0) % tt == 0
    prev = jnp.where(is_first, first, pltpu.roll(h, 1, axis=0))
    last = h3[:, tt - 1:tt, :]
    hs_ref[...] = last
    sh_o[...] = last

    xx = prev - h
    mix = mix_ref[...]
    xr, xw, xk, xv, xa, xg = ((h + xx * mix[i:i + 1, :]).astype(BF16) for i in range(6))

    r = jnp.dot(xr, wr_ref[...], preferred_element_type=F32)
    k = jnp.dot(xk, wk_ref[...], preferred_element_type=F32)
    v = jnp.dot(xv, wv_ref[...], preferred_element_type=F32)
    w_pre = w0_ref[...] + _bdot(jnp.tanh(jnp.dot(xw, w1_ref[...], preferred_element_type=F32)), w2_ref[...])
    lw = -math.exp(-0.5) * jax.nn.sigmoid(w_pre)
    if has_vfirst:
        gate = jax.nn.sigmoid(v0_ref[...] + _bdot(jnp.dot(xv, v1_ref[...], preferred_element_type=F32), v2_ref[...]))
        v = v + (vf_ref[...].reshape(m, d) - v) * gate
    a = jax.nn.sigmoid(a0_ref[...] + _bdot(jnp.dot(xa, a1_ref[...], preferred_element_type=F32), a2_ref[...]))
    g = _bdot(jax.nn.sigmoid(jnp.dot(xg, g1_ref[...], preferred_element_type=F32)), g2_ref[...])
    kk = k * kk_ref[...]
    nrm = jnp.maximum(jnp.sqrt(_segsum(kk * kk, bd_ref[...], 3)), L2_EPS)
    kk = kk / nrm
    k = k * (1.0 + (a - 1.0) * ka_ref[...])

    shp = (bb, tt, d)
    r_o[...] = r.reshape(shp)
    lw_o[...] = lw.reshape(shp)
    k_o[...] = k.reshape(shp)
    v_o[...] = v.reshape(shp)
    kkn_o[...] = kk.reshape(shp)
    b_o[...] = (kk * a).reshape(shp)
    g_o[...] = g.reshape(shp).astype(g_o.dtype)


def _rwkv_proj(x, shift0, g, p, ri, v_first, bd, *, bb, tt, gate_dtype):
    b, t, d = x.shape
    vec = lambda a: a.reshape(1, d)
    tile = pl.BlockSpec((bb, tt, d), lambda i, j: (i, j, 0))
    row = pl.BlockSpec((bb, 1, d), lambda i, j: (i, 0, 0))
    args = [x, shift0.reshape(b, 1, d), vec(g), p['rw_mix'][ri],
            p['rw_w_r'][ri], p['rw_w_k'][ri], p['rw_w_v'][ri],
            vec(p['rw_w0'][ri]), p['rw_w1'][ri], p['rw_w2'][ri],
            vec(p['rw_a0'][ri]), p['rw_a1'][ri], p['rw_a2'][ri],
            p['rw_g1'][ri], p['rw_g2'][ri], vec(p['rw_k_k'][ri]), vec(p['rw_k_a'][ri]), bd]
    in_specs = [tile, row] + [_resident(a.shape) for a in args[2:]]
    has_vfirst = v_first is not None
    if has_vfirst:
        vi = ri - 1
        extra = [v_first, vec(p['rw_v0'][vi]), p['rw_v1'][vi], p['rw_v2'][vi]]
        args += extra
        in_specs += [tile] + [_resident(a.shape) for a in extra[1:]]
    big = jax.ShapeDtypeStruct((b, t, d), F32)
    gate = jax.ShapeDtypeStruct((b, t, d), gate_dtype)
    outs = pl.pallas_call(
        functools.partial(_rwkv_proj_kernel, has_vfirst=has_vfirst),
        grid=(b // bb, t // tt),
        in_specs=in_specs,
        out_specs=[tile] * 7 + [row],
        out_shape=[big] * 6 + [gate, jax.ShapeDtypeStruct((b, 1, d), F32)],
        scratch_shapes=[pltpu.VMEM((bb, 1, d), F32)],
        compiler_params=_params(2),
        name="rwkv_proj",
    )(*args)
    return outs


def _scan_kernel(r_ref, lw_ref, k_ref, v_ref, kk_ref, b_ref, s0_ref, tri_ref, y_ref, so_ref, s_scr, *, tc, C):
    bb, _, d = r_ref.shape
    n = RWKV_HEAD
    n_pairs = d // LANES
    t = pl.program_id(1)

    @pl.when(t == 0)
    def _():
        for i in range(bb):
            for p in range(n_pairs):
                s_scr[i, p] = jnp.concatenate([s0_ref[i, 2 * p], s0_ref[i, 2 * p + 1]], axis=-1)

    def load(ref, i):
        x = ref[i]
        if tc < C:
            x = jnp.concatenate([x, jnp.zeros((C - tc, d), F32)], axis=0)
        return x

    lane_lo = lax.broadcasted_iota(jnp.int32, (1, LANES), 1) < n
    row = lax.broadcasted_iota(jnp.int32, (2 * C, 2 * C), 0)
    col = lax.broadcasted_iota(jnp.int32, (2 * C, 2 * C), 1)
    row_hi = row >= C
    col_hi = col >= C
    tr = jnp.where(row_hi, row - C, row)
    tj = jnp.where(col_hi, col - C, col)
    diag = row_hi == col_hi
    m_l = jnp.logical_and(diag, tj < tr)
    m_ak = jnp.logical_and(jnp.logical_not(diag), tj < tr)
    m_incl = tj <= tr
    eye = (row == col).astype(F32)
    rl_row = lax.broadcasted_iota(jnp.int32, (2 * C, LANES), 0) >= C
    rl_lane = lax.broadcasted_iota(jnp.int32, (2 * C, LANES), 1) >= n
    m_rl = rl_row == rl_lane

    def stack_masked(x):
        return jnp.concatenate([jnp.where(lane_lo, x, 0.0), jnp.where(lane_lo, 0.0, x)], axis=0).astype(BF16)

    units = []
    for i in range(bb):
        r, lw, k, v, kk, b = (load(ref, i) for ref in (r_ref, lw_ref, k_ref, v_ref, kk_ref, b_ref))
        cl = _split_dot_lhs(tri_ref[...], lw, 3)
        cl_end = cl[C - 1:C, :]
        e_neg = jnp.exp(-cl)
        e_end = jnp.exp(cl_end - cl)
        a_hat = -kk * jnp.exp(cl - lw)
        r_hat = r * jnp.exp(cl)
        b_hat = b * e_neg
        k_hat = k * e_neg
        b_bar = b * e_end
        k_bar = k * e_end
        w_end = jnp.exp(cl_end)
        for p in range(n_pairs):
            sl = slice(p * LANES, (p + 1) * LANES)
            units.append(dict(i=i, p=p, sl=sl, vv=v[:, sl], a_hat=a_hat[:, sl], r_hat=r_hat[:, sl],
                              b_hat=b_hat[:, sl], k_hat=k_hat[:, sl], b_bar=b_bar[:, sl], k_bar=k_bar[:, sl],
                              w_end=w_end[:, sl]))

    for u in units:
        bh, kh = u['b_hat'], u['k_hat']
        u['s_old'] = s_scr[u['i'], u['p']]
        rhs = jnp.concatenate([u['s_old'], u['s_old'], jnp.where(lane_lo, bh, kh), jnp.where(lane_lo, kh, bh)],
                              axis=0).astype(BF16)
        lhs = jnp.concatenate([stack_masked(u['a_hat']), stack_masked(u['r_hat'])], axis=0)
        gp = lax.dot_general(lhs, rhs, NT, preferred_element_type=F32)
        u['p_a'], g_a = gp[:2 * C, :LANES], gp[:2 * C, LANES:]
        u['p_r'], g_r = gp[2 * C:, :LANES], gp[2 * C:, LANES:]
        u['l_mat'] = jnp.where(m_l, g_a, 0.0)
        u['t_mat'] = eye + u['l_mat']
        u['ak'] = jnp.where(m_ak, g_a, 0.0)
        u['g_r'] = jnp.where(m_incl, g_r, 0.0)

    n_factors = int(math.log2(C)) - 1
    for u in units:
        u['l_pow'] = _bdot(u['l_mat'], u['l_mat'])
    for _ in range(n_factors - 1):
        for u in units:
            both = _bdot(jnp.concatenate([u['t_mat'], u['l_pow']], axis=0), u['l_pow'])
            u['t_mat'] = u['t_mat'] + both[:2 * C]
            u['l_pow'] = both[2 * C:]
    for u in units:
        u['t_mat'] = u['t_mat'] + _bdot(u['t_mat'], u['l_pow'])

    for u in units:
        vv2 = jnp.concatenate([u['vv'], u['vv']], axis=0)
        u['rhs_u'] = jnp.where(m_rl, u['p_a'] + _bdot(u['ak'], vv2), 0.0)
    for u in units:
        u_st = _bdot(u['t_mat'], u['rhs_u'])
        u['u_lo'], u['u_hi'] = u_st[:C], u_st[C:]
    for u in units:
        vv = u['vv']
        w_mix = jnp.concatenate([jnp.where(lane_lo, u['u_lo'], vv), jnp.where(lane_lo, vv, u['u_hi'])], axis=0)
        y_st = jnp.where(m_rl, u['p_r'] + _bdot(u['g_r'], w_mix), 0.0)
        y = y_st[:C] + y_st[C:]
        y_ref[u['i'], :, u['sl']] = y[:tc]
    for u in units:
        uv_t = jnp.concatenate([u['u_lo'] + u['u_hi'], u['vv']], axis=0).T
        bk = jnp.concatenate([u['b_bar'], u['k_bar']], axis=0)
        full = _bdot(uv_t, bk)
        s_new = u['s_old'] * u['w_end'] + jnp.where(lane_lo, full[:n], full[n:])
        s_scr[u['i'], u['p']] = s_new

    @pl.when(t == pl.num_programs(1) - 1)
    def _():
        for i in range(bb):
            for p in range(n_pairs):
                s_fin = s_scr[i, p]
                so_ref[i, 2 * p] = s_fin[:, :n]
                so_ref[i, 2 * p + 1] = s_fin[:, n:]


def _rwkv_scan(r, lw, k, v, kk, b, s_all, ri, carry, tri, *, bb, tc):
    bsz, t, d = r.shape
    n_pairs = d // LANES
    chunk = tri.shape[0]
    tile = pl.BlockSpec((bb, tc, d), lambda i, j: (i, j, 0))
    st = pl.BlockSpec((None, bb) + s_all.shape[2:], lambda i, j: (ri, i, 0, 0, 0))
    args = [r, lw, k, v, kk, b, s_all, tri]
    in_specs = [tile] * 6 + [st, _resident(tri.shape)]
    body = functools.partial(_scan_kernel, tc=tc, C=chunk)
    aliases = {}
    if carry is not None:
        args.append(carry)
        in_specs.append(pl.BlockSpec(memory_space=pl.ANY))
        aliases = {len(args) - 1: 1}
        body = _drop_ref(body, len(args) - 1)
    return pl.pallas_call(
        body,
        grid=(bsz // bb, t // tc),
        in_specs=in_specs,
        out_specs=[tile, st],
        out_shape=[jax.ShapeDtypeStruct((bsz, t, d), F32), jax.ShapeDtypeStruct(s_all.shape, F32)],
        scratch_shapes=[pltpu.VMEM((bb, n_pairs, RWKV_HEAD, LANES), F32)],
        input_output_aliases=aliases,
        compiler_params=_params(2),
        name="rwkv_scan",
    )(*args)


def _rwkv_out_kernel(y_ref, r_ref, k_ref, v_ref, g_ref, x_ref, lg_ref, lb_ref, rk_ref, wo_ref, bd_ref, o_ref):
    bd = bd_ref[...]
    inv_n = 1.0 / RWKV_HEAD
    y = y_ref[...]
    yc = y - _segsum(y, bd, 2) * inv_n
    var = _segsum(yc * yc, bd, 2) * inv_n
    yn = yc * lax.rsqrt(var + LNX_EPS) * lg_ref[...] + lb_ref[...]
    v = v_ref[...]
    bonus = _segsum(r_ref[...] * k_ref[...] * rk_ref[...], bd, 2) * v
    z = ((yn + bonus) * g_ref[...]).astype(BF16)
    o_ref[...] = x_ref[...] + jnp.dot(z, wo_ref[...], preferred_element_type=F32)


def _rwkv_out(y, r, k, v, g, x, lnx_g, lnx_b, r_k, w_o, bd, *, tm=256):
    m, d = x.shape
    tm = min(tm, m)
    tile = pl.BlockSpec((tm, d), lambda i: (i, 0))
    vec = lambda a: a.reshape(1, d)
    return pl.pallas_call(
        _rwkv_out_kernel,
        grid=(m // tm,),
        in_specs=[tile] * 6 + [_resident((1, d))] * 3 + [_resident(w_o.shape), _resident(bd.shape)],
        out_specs=tile,
        out_shape=jax.ShapeDtypeStruct((m, d), F32),
        compiler_params=_params(1),
        name="rwkv_out",
    )(y, r, k, v, g, x, vec(lnx_g), vec(lnx_b), vec(r_k), w_o, bd)


def _trunk(x, mem_k, mem_v, conv_state, shift_state, wkv_state, p, consts, cfg, mem_heads=None):
    bsz, t, d = x.shape
    m = bsz * t
    depth = p['norm_g'].shape[0]
    bd = consts
    ar = jnp.arange(cfg['scan_chunk'])
    tri = (ar[None, :] <= ar[:, None]).astype(BF16)
    x2 = x.reshape(m, d)
    conv_new, wkv_new, shift_new = None, None, []
    v_first = None
    for layer in range(depth):
        g = p['norm_g'][layer]
        if layer % 2 == 0:
            ci = layer // 2
            x3, conv_new = _conv_mixer(x2.reshape(bsz, t, d), g[0], conv_state, ci, conv_new, p,
                                       bb=cfg['conv_bb'], tt=cfg['conv_tt'])
            x2 = x3.reshape(m, d)
        else:
            ri = layer // 2
            r, lw, k, v, kk, b, gt, sh = _rwkv_proj(x2.reshape(bsz, t, d), shift_state[ri], g[0], p, ri, v_first, bd,
                                                    bb=cfg['proj_bb'], tt=cfg['proj_tt'],
                                                    gate_dtype=cfg['gate_dtype'])
            if v_first is None:
                v_first = v
            y, wkv_new = _rwkv_scan(r, lw, k, v, kk, b, wkv_state, ri, wkv_new, tri,
                                    bb=cfg['scan_bb'], tc=cfg['scan_tc'])
            shift_new.append(sh.reshape(bsz, d))
            flat = lambda a: a.reshape(m, d)
            x2 = _rwkv_out(flat(y), flat(r), flat(k), flat(v), flat(gt), x2, p['rw_lnx_g'][ri], p['rw_lnx_b'][ri],
                           p['rw_r_k'][ri].reshape(d), p['rw_w_o'][ri], bd)
        last = layer == depth - 1
        if mem_heads is not None:
            x2 = _xattn_mlp_fused(x2, g[1], p['xa_w_q'][layer], mem_heads[0], mem_heads[1], layer,
                                  p['xa_w_o'][layer], g[2], p['mlp_w_up'][layer], p['mlp_w_down'][layer],
                                  p['final_g'], seq_len=t, tt=cfg['attn_tt'], final_norm=last)
        else:
            q = _norm_matmul(x2, g[1], p['xa_w_q'][layer], out_dtype=cfg['attn_dtype'])
            o = _attn(q.reshape(bsz, t, d), mem_k, mem_v, layer, bb=cfg['attn_bb'], tt=cfg['attn_tt'],
                      out_dtype=cfg['attn_dtype'])
            x2 = _matmul_res(o.reshape(m, d), p['xa_w_o'][layer], x2)
            x2 = _mlp(x2, g[2], p['mlp_w_up'][layer], p['mlp_w_down'][layer], p['final_g'], final_norm=last)
    return x2.reshape(bsz, t, d), conv_new, jnp.stack(shift_new), wkv_new


PROMPT_CFG = dict(conv_bb=1, conv_tt=256, proj_bb=1, proj_tt=256, gate_dtype=BF16, scan_bb=4, scan_tc=SCAN_CHUNK,
                  scan_chunk=SCAN_CHUNK, attn_bb=1, attn_tt=512, attn_dtype=BF16)
SAMPLE_CFG = dict(conv_bb=16, conv_tt=8, proj_bb=16, proj_tt=8, gate_dtype=F32, scan_bb=4, scan_tc=8,
                  scan_chunk=8, attn_bb=4, attn_tt=8, attn_dtype=F32)


def kernel(x_prompt, x_sample, mem_prompt, cache_mem_k, cache_mem_v, state_conv, state_shift, state_wkv, norm_g, final_g, cv_w_in, cv_b_in, cv_w_dw, cv_b_dw, cv_ln_g, cv_ln_b, cv_w_out, cv_b_out, rw_mix, rw_w_r, rw_w_k, rw_w_v, rw_w_o, rw_w0, rw_w1, rw_w2, rw_a0, rw_a1, rw_a2, rw_v0, rw_v1, rw_v2, rw_g1, rw_g2, rw_k_k, rw_k_a, rw_r_k, rw_lnx_g, rw_lnx_b, xa_w_q, xa_w_kv, xa_w_o, mlp_w_up, mlp_w_down):
    bf = lambda w: w.astype(BF16)
    p = {
        'norm_g': norm_g, 'final_g': final_g,
        'cv_w_in': bf(cv_w_in), 'cv_b_in': cv_b_in, 'cv_w_dw': cv_w_dw, 'cv_b_dw': cv_b_dw,
        'cv_ln_g': cv_ln_g, 'cv_ln_b': cv_ln_b, 'cv_w_out': bf(cv_w_out), 'cv_b_out': cv_b_out,
        'rw_mix': rw_mix, 'rw_w_r': bf(rw_w_r), 'rw_w_k': bf(rw_w_k), 'rw_w_v': bf(rw_w_v), 'rw_w_o': bf(rw_w_o),
        'rw_w0': rw_w0, 'rw_w1': bf(rw_w1), 'rw_w2': bf(rw_w2), 'rw_a0': rw_a0, 'rw_a1': bf(rw_a1),
        'rw_a2': bf(rw_a2), 'rw_v0': rw_v0, 'rw_v1': bf(rw_v1), 'rw_v2': bf(rw_v2), 'rw_g1': bf(rw_g1),
        'rw_g2': bf(rw_g2), 'rw_k_k': rw_k_k, 'rw_k_a': rw_k_a, 'rw_r_k': rw_r_k,
        'rw_lnx_g': rw_lnx_g, 'rw_lnx_b': rw_lnx_b,
        'xa_w_q': bf(xa_w_q), 'xa_w_o': bf(xa_w_o), 'mlp_w_up': bf(mlp_w_up), 'mlp_w_down': bf(mlp_w_down),
    }
    depth = norm_g.shape[0]
    bp, mlen, d = mem_prompt.shape
    n_conv, n_rwkv = state_conv.shape[0], state_wkv.shape[0]
    heads, n = state_wkv.shape[2], state_wkv.shape[3]
    hd = d // XA_HEADS

    idx = jnp.arange(SEG_TILE) // RWKV_HEAD
    bd = (idx[:, None] == idx[None, :]).astype(BF16)
    consts = bd

    mem_k_prompt, mem_v_prompt, mem_kh, mem_vh = _memkv(mem_prompt, bf(xa_w_kv))

    conv0 = jnp.zeros((n_conv, bp, CONV_STATE, d), F32)
    shift0 = jnp.zeros((n_rwkv, bp, d), F32)
    wkv0 = jnp.zeros((n_rwkv, bp, heads, n, n), F32)
    y_prompt, conv_prompt, shift_prompt, wkv_prompt = _trunk(
        x_prompt, mem_k_prompt, mem_v_prompt, conv0, shift0, wkv0, p, consts, PROMPT_CFG,
        mem_heads=(mem_kh, mem_vh))
    y_sample, conv_sample, shift_sample, wkv_sample = _trunk(
        x_sample, cache_mem_k, cache_mem_v, state_conv, state_shift, state_wkv, p, consts, SAMPLE_CFG)
    return (y_prompt, y_sample, mem_k_prompt, mem_v_prompt, conv_prompt, shift_prompt, wkv_prompt,
            conv_sample, shift_sample, wkv_sample)
```

```python
import functools
import math

import jax
import jax.numpy as jnp
from jax import lax
from jax.experimental import pallas as pl
from jax.experimental.pallas import tpu as pltpu

F32 = jnp.float32
BF16 = jnp.bfloat16

RMS_EPS = 1e-6
LN_EPS = 1e-5
L2_EPS = 1e-12
LNX_EPS = 64e-5
CONV_WIDTH = 31
CONV_STATE = CONV_WIDTH - 1
CONV_PAD = 32
RWKV_HEAD = 64
XA_HEADS = 4
SCAN_CHUNK = 64
LANES = 128
SUBLANES = 8
SEG_TILE = 256
VMEM_LIMIT = 56 * 1024 * 1024

NT = (((1,), (1,)), ((), ()))


def _params(n_axes, sem=None):
    if sem is None:
        sem = ("arbitrary",) * n_axes
    return pltpu.CompilerParams(dimension_semantics=sem, vmem_limit_bytes=VMEM_LIMIT)


def _resident(shape):
    nd = len(shape)
    return pl.BlockSpec(tuple(shape), lambda *_: (0,) * nd, pipeline_mode=pl.Buffered(1))


def _drop_ref(body, pos):
    def wrapped(*refs):
        return body(*refs[:pos], *refs[pos + 1:])
    return wrapped


def _rms(x, g):
    ms = jnp.mean(x * x, axis=-1, keepdims=True)
    return x * lax.rsqrt(ms + RMS_EPS) * g


def _bdot(a, b):
    return jnp.dot(a.astype(BF16), b.astype(BF16), preferred_element_type=F32)


def _split_dot_rhs(x, w, passes):
    acc = None
    rem = x
    for i in range(passes):
        part = rem.astype(BF16)
        term = jnp.dot(part, w, preferred_element_type=F32)
        acc = term if acc is None else acc + term
        if i + 1 < passes:
            rem = rem - part.astype(F32)
    return acc


def _split_dot_lhs(w, x, passes):
    acc = None
    rem = x
    for i in range(passes):
        part = rem.astype(BF16)
        term = jnp.dot(w, part, preferred_element_type=F32)
        acc = term if acc is None else acc + term
        if i + 1 < passes:
            rem = rem - part.astype(F32)
    return acc


def _segsum(x, bd, passes):
    d = x.shape[-1]
    outs = []
    for c in range(d // SEG_TILE):
        outs.append(_split_dot_rhs(x[:, c * SEG_TILE:(c + 1) * SEG_TILE], bd, passes))
    return jnp.concatenate(outs, axis=-1)


def _norm_matmul_kernel(*refs, glu, has_bias):
    if has_bias:
        x_ref, g_ref, w_ref, b_ref, o_ref = refs
    else:
        x_ref, g_ref, w_ref, o_ref = refs
    h = _rms(x_ref[...], g_ref[...]).astype(BF16)
    r = jnp.dot(h, w_ref[...], preferred_element_type=F32)
    if has_bias:
        r = r + b_ref[...]
    if glu:
        d = r.shape[-1] // 2
        r = r[:, :d] * jax.nn.sigmoid(r[:, d:])
    o_ref[...] = r.astype(o_ref.dtype)


def _norm_matmul(x, g, w, b=None, *, glu=False, out_dtype=F32, tm=512):
    m, d = x.shape
    n = w.shape[1]
    n_out = n // 2 if glu else n
    tm = min(tm, m)
    in_specs = [pl.BlockSpec((tm, d), lambda i: (i, 0)), _resident((1, d)), _resident((d, n))]
    args = [x, g.reshape(1, d), w]
    if b is not None:
        in_specs.append(_resident((1, n)))
        args.append(b.reshape(1, n))
    return pl.pallas_call(
        functools.partial(_norm_matmul_kernel, glu=glu, has_bias=b is not None),
        grid=(m // tm,),
        in_specs=in_specs,
        out_specs=pl.BlockSpec((tm, n_out), lambda i: (i, 0)),
        out_shape=jax.ShapeDtypeStruct((m, n_out), out_dtype),
        compiler_params=_params(1),
        name="norm_matmul",
    )(*args)


def _matmul_res_kernel(*refs, has_bias):
    if has_bias:
        a_ref, w_ref, b_ref, x_ref, o_ref = refs
    else:
        a_ref, w_ref, x_ref, o_ref = refs
    r = jnp.dot(a_ref[...].astype(BF16), w_ref[...], preferred_element_type=F32)
    if has_bias:
        r = r + b_ref[...]
    o_ref[...] = x_ref[...] + r


def _matmul_res(a, w, x, b=None, *, tm=512):
    m, k = a.shape
    d = w.shape[1]
    tm = min(tm, m)
    in_specs = [pl.BlockSpec((tm, k), lambda i: (i, 0)), _resident((k, d))]
    args = [a, w]
    if b is not None:
        in_specs.append(_resident((1, d)))
        args.append(b.reshape(1, d))
    in_specs.append(pl.BlockSpec((tm, d), lambda i: (i, 0)))
    args.append(x)
    return pl.pallas_call(
        functools.partial(_matmul_res_kernel, has_bias=b is not None),
        grid=(m // tm,),
        in_specs=in_specs,
        out_specs=pl.BlockSpec((tm, d), lambda i: (i, 0)),
        out_shape=jax.ShapeDtypeStruct((m, d), F32),
        compiler_params=_params(1),
        name="matmul_res",
    )(*args)


MLP_CHUNKS = 4


def _mlp_block(x, g_ref, wu_ref, wd_ref, fg_ref, final_norm):
    h = _rms(x, g_ref[...]).astype(BF16)
    ck = wu_ref.shape[1] // MLP_CHUNKS
    acc = x
    for c in range(MLP_CHUNKS):
        u = jnp.dot(h, wu_ref[:, c * ck:(c + 1) * ck], preferred_element_type=F32)
        u = jnp.maximum(u, 0.0)
        u = (u * u).astype(BF16)
        acc = acc + jnp.dot(u, wd_ref[c * ck:(c + 1) * ck, :], preferred_element_type=F32)
    if final_norm:
        acc = _rms(acc, fg_ref[...])
    return acc


def _mlp_kernel(x_ref, g_ref, wu_ref, wd_ref, fg_ref, o_ref, *, final_norm):
    o_ref[...] = _mlp_block(x_ref[...], g_ref, wu_ref, wd_ref, fg_ref, final_norm)


def _mlp(x, g, w_up, w_down, final_g, *, final_norm, tm=512):
    m, d = x.shape
    dff = w_up.shape[1]
    tm = min(tm, m)
    return pl.pallas_call(
        functools.partial(_mlp_kernel, final_norm=final_norm),
        grid=(m // tm,),
        in_specs=[pl.BlockSpec((tm, d), lambda i: (i, 0)), _resident((1, d)),
                  _resident((d, dff)), _resident((dff, d)), _resident((1, d))],
        out_specs=pl.BlockSpec((tm, d), lambda i: (i, 0)),
        out_shape=jax.ShapeDtypeStruct((m, d), F32),
        compiler_params=_params(1),
        name="mlp",
    )(x, g.reshape(1, d), w_up, w_down, final_g.reshape(1, d))


def _memkv_kernel(x_ref, w_ref, k_ref, v_ref, kh_ref, vh_ref):
    r = jnp.dot(x_ref[...].astype(BF16), w_ref[0], preferred_element_type=F32)
    tb, mlen, heads, hd = k_ref.shape
    d = heads * hd
    for s in range(tb):
        rows = slice(s * mlen, (s + 1) * mlen)
        for h in range(heads):
            kb = r[rows, h * hd:(h + 1) * hd]
            vb = r[rows, d + h * hd:d + (h + 1) * hd]
            k_ref[s, :, h, :] = kb
            v_ref[s, :, h, :] = vb
            kh_ref[s, h] = kb.astype(BF16)
            vh_ref[s, h] = vb.astype(BF16)


def _memkv(mem, w_kv, *, tb=2):
    bp, mlen, d = mem.shape
    depth = w_kv.shape[0]
    hd = d // XA_HEADS
    out = jax.ShapeDtypeStruct((depth, bp, mlen, XA_HEADS, hd), F32)
    out_h = jax.ShapeDtypeStruct((depth, bp, XA_HEADS, mlen, hd), BF16)
    blk = pl.BlockSpec((None, tb, mlen, XA_HEADS, hd), lambda l, i: (l, i, 0, 0, 0))
    blk_h = pl.BlockSpec((None, tb, XA_HEADS, mlen, hd), lambda l, i: (l, i, 0, 0, 0))
    return pl.pallas_call(
        _memkv_kernel,
        grid=(depth, bp // tb),
        in_specs=[pl.BlockSpec((tb * mlen, d), lambda l, i: (i, 0)),
                  pl.BlockSpec((1, d, 2 * d), lambda l, i: (l, 0, 0))],
        out_specs=[blk, blk, blk_h, blk_h],
        out_shape=[out, out, out_h, out_h],
        compiler_params=_params(2),
        name="memkv",
    )(mem.reshape(bp * mlen, d), w_kv)


def _xattn_mlp_kernel(*refs, scale, final_norm, has_pre):
    if has_pre:
        pre_ref, wpre_ref = refs[:2]
        refs = refs[2:]
    x_ref, ga_ref, wq_ref, k_ref, v_ref, wo_ref, gm_ref, wu_ref, wd_ref, fg_ref, o_ref = refs
    heads, _, hd = k_ref.shape
    x = x_ref[...]
    if has_pre:
        x = x + jnp.dot(pre_ref[...], wpre_ref[...], preferred_element_type=F32)
    q = jnp.dot(_rms(x, ga_ref[...]).astype(BF16), wq_ref[...], preferred_element_type=F32).astype(BF16)
    outs = []
    for h in range(heads):
        s = lax.dot_general(q[:, h * hd:(h + 1) * hd], k_ref[h], NT, preferred_element_type=F32) * scale
        e = jnp.exp(s - jnp.max(s, axis=-1, keepdims=True))
        p = (e / jnp.sum(e, axis=-1, keepdims=True)).astype(BF16)
        outs.append(jnp.dot(p, v_ref[h], preferred_element_type=F32).astype(BF16))
    o = jnp.concatenate(outs, axis=-1)
    x = x + jnp.dot(o, wo_ref[...], preferred_element_type=F32)
    o_ref[...] = _mlp_block(x, gm_ref, wu_ref, wd_ref, fg_ref, final_norm)


def _xattn_mlp_fused(x2, pre, g_attn, w_q, kh, vh, layer, w_o, g_mlp, w_up, w_down, final_g, *, seq_len, tt,
                     final_norm):
    m, d = x2.shape
    _, _, heads, mlen, hd = kh.shape
    per_seq = seq_len // tt
    kv = pl.BlockSpec((None, None, heads, mlen, hd), lambda i: (layer, i // per_seq, 0, 0, 0))
    tile = pl.BlockSpec((tt, d), lambda i: (i, 0))
    vec = lambda a: a.reshape(1, d)
    args = [x2, vec(g_attn), w_q, kh, vh, w_o, vec(g_mlp), w_up, w_down, vec(final_g)]
    in_specs = [tile, _resident((1, d)), _resident(w_q.shape), kv, kv, _resident(w_o.shape),
                _resident((1, d)), _resident(w_up.shape), _resident(w_down.shape), _resident((1, d))]
    if pre is not None:
        args = [pre[0], pre[1]] + args
        in_specs = [tile, _resident(pre[1].shape)] + in_specs
    return pl.pallas_call(
        functools.partial(_xattn_mlp_kernel, scale=hd ** -0.5, final_norm=final_norm, has_pre=pre is not None),
        grid=(m // tt,),
        in_specs=in_specs,
        out_specs=tile,
        out_shape=jax.ShapeDtypeStruct((m, d), F32),
        compiler_params=_params(1),
        name="xattn_mlp_fused",
    )(*args)


def _attn_kernel(q_ref, k_ref, v_ref, o_ref, *, scale):
    bb, tt, _ = q_ref.shape
    m, heads, hd = k_ref.shape[1], k_ref.shape[2], k_ref.shape[3]
    row_head = lax.broadcasted_iota(jnp.int32, (heads * tt, m * heads), 0) // tt
    col_head = lax.broadcasted_iota(jnp.int32, (heads * tt, m * heads), 1) % heads
    own = row_head == col_head
    for i in range(bb):
        k2 = k_ref[i].reshape(m * heads, hd).astype(BF16)
        v2 = v_ref[i].reshape(m * heads, hd).astype(BF16)
        q2 = jnp.concatenate([q_ref[i, :, h * hd:(h + 1) * hd] for h in range(heads)], axis=0).astype(BF16)
        s = lax.dot_general(q2, k2, NT, preferred_element_type=F32) * scale
        s = jnp.where(own, s, -jnp.inf)
        e = jnp.exp(s - jnp.max(s, axis=-1, keepdims=True))
        p = (e / jnp.sum(e, axis=-1, keepdims=True)).astype(BF16)
        o2 = jnp.dot(p, v2, preferred_element_type=F32)
        o_ref[i] = jnp.concatenate([o2[h * tt:(h + 1) * tt] for h in range(heads)], axis=-1).astype(o_ref.dtype)


def _attn(q, mem_k, mem_v, layer, *, bb, tt, out_dtype):
    b, t, d = q.shape
    _, _, m, heads, hd = mem_k.shape
    kv = pl.BlockSpec((None, bb, m, heads, hd), lambda i, j: (layer, i, 0, 0, 0))
    return pl.pallas_call(
        functools.partial(_attn_kernel, scale=hd ** -0.5),
        grid=(b // bb, t // tt),
        in_specs=[pl.BlockSpec((bb, tt, d), lambda i, j: (i, j, 0)), kv, kv],
        out_specs=pl.BlockSpec((bb, tt, d), lambda i, j: (i, j, 0)),
        out_shape=jax.ShapeDtypeStruct((b, t, d), out_dtype),
        compiler_params=_params(2),
        name="attn_core",
    )(q, mem_k, mem_v)


def _conv_kernel(x_ref, c0_ref, g_ref, win_ref, bin_ref, f_ref, bdw_ref, lg_ref, lb_ref, wout_ref, bout_ref,
                 o_ref, cs_ref, up_ref, *, lane_chunk, slab):
    bb, tt, d = x_ref.shape
    t = pl.program_id(1)
    lo = CONV_PAD - CONV_STATE
    x = x_ref[...].reshape(bb * tt, d)
    pre = jnp.dot(_rms(x, g_ref[...]).astype(BF16), win_ref[...], preferred_element_type=F32) + bin_ref[...]
    u = (pre[:, :d] * jax.nn.sigmoid(pre[:, d:])).reshape(bb, tt, d)

    @pl.when(t == 0)
    def _():
        up_ref[:, 0:lo, :] = jnp.zeros((bb, lo, d), F32)
        up_ref[:, CONV_PAD + tt:, :] = jnp.zeros((bb, SUBLANES, d), F32)
        up_ref[:, lo:CONV_PAD, :] = c0_ref[...]

    up_ref[:, CONV_PAD:CONV_PAD + tt, :] = u
    acts = []
    for i in range(bb):
        zs = []
        for c in range(d // lane_chunk):
            cs = slice(c * lane_chunk, (c + 1) * lane_chunk)
            acc = None
            for o in range(SUBLANES):
                g = None
                for j in range(CONV_WIDTH):
                    if (lo + j) % SUBLANES != o:
                        continue
                    base = lo + j - o
                    term = up_ref[i, base:base + tt + SUBLANES, cs] * f_ref[j:j + 1, cs]
                    g = term if g is None else g + term
                shifted = g[o:o + tt]
                acc = shifted if acc is None else acc + shifted
            zs.append(acc)
        z = jnp.concatenate(zs, axis=-1) + bdw_ref[...]
        zc = z - jnp.mean(z, axis=-1, keepdims=True)
        zn = zc * lax.rsqrt(jnp.mean(zc * zc, axis=-1, keepdims=True) + LN_EPS)
        zn = zn * lg_ref[...] + lb_ref[...]
        acts.append((zn * jax.nn.sigmoid(zn)).astype(BF16))
    act = acts[0] if bb == 1 else jnp.concatenate(acts, axis=0)
    out = x + jnp.dot(act, wout_ref[...], preferred_element_type=F32) + bout_ref[...]
    o_ref[...] = out.reshape(bb, tt, d)
    tail = up_ref[:, tt + lo:tt + CONV_PAD, :]
    if slab is None:
        cs_ref[...] = tail
    else:
        for other in range(cs_ref.shape[0]):
            cs_ref[other] = tail if other == slab else jnp.zeros_like(tail)
    up_ref[:, lo:CONV_PAD, :] = tail


def _conv_mixer(x, g, conv_all, ci, carry, p, *, bb, tt):
    b, t, d = x.shape
    vec = lambda a: a.reshape(1, -1)
    slab = pl.BlockSpec((None, bb, CONV_STATE, d), lambda i, j: (ci, i, 0, 0))
    args = [x, conv_all, vec(g), p['cv_w_in'][ci], vec(p['cv_b_in'][ci]), p['cv_w_dw'][ci], vec(p['cv_b_dw'][ci]),
            vec(p['cv_ln_g'][ci]), vec(p['cv_ln_b'][ci]), p['cv_w_out'][ci], vec(p['cv_b_out'][ci])]
    in_specs = [pl.BlockSpec((bb, tt, d), lambda i, j: (i, j, 0)), slab] + [_resident(a.shape) for a in args[2:]]
    aliases = {}
    if carry is None:
        body = functools.partial(_conv_kernel, lane_chunk=2 * LANES, slab=ci)
        slab_out = pl.BlockSpec((conv_all.shape[0], bb, CONV_STATE, d), lambda i, j: (0, i, 0, 0))
    else:
        args.append(carry)
        in_specs.append(pl.BlockSpec(memory_space=pl.ANY))
        aliases = {len(args) - 1: 1}
        body = _drop_ref(functools.partial(_conv_kernel, lane_chunk=2 * LANES, slab=None), len(args) - 1)
        slab_out = slab
    return pl.pallas_call(
        body,
        grid=(b // bb, t // tt),
        in_specs=in_specs,
        out_specs=[pl.BlockSpec((bb, tt, d), lambda i, j: (i, j, 0)), slab_out],
        out_shape=[jax.ShapeDtypeStruct((b, t, d), F32), jax.ShapeDtypeStruct(conv_all.shape, F32)],
        scratch_shapes=[pltpu.VMEM((bb, CONV_PAD + tt + SUBLANES, d), F32)],
        input_output_aliases=aliases,
        compiler_params=_params(2),
        name="conv_core",
    )(*args)


def _rwkv_proj_kernel(*refs, has_vfirst):
    (x_ref, sh0_ref, g_ref, mix_ref, wr_ref, wk_ref, wv_ref, w0_ref, w1_ref, w2_ref,
     a0_ref, a1_ref, a2_ref, g1_ref, g2_ref, kk_ref, ka_ref, bd_ref) = refs[:18]
    pos = 18
    if has_vfirst:
        vf_ref, v0_ref, v1_ref, v2_ref = refs[pos:pos + 4]
        pos += 4
    (r_o, lw_o, k_o, v_o, kkn_o, b_o, g_o, sh_o) = refs[pos:pos + 8]
    hs_ref = refs[pos + 8]

    bb, tt, d = x_ref.shape
    t = pl.program_id(1)
    h3 = _rms(x_ref[...], g_ref[...])

    @pl.when(t == 0)
    def _():
        hs_ref[...] = sh0_ref[...]

    m = bb * tt
    h = h3.reshape(m, d)
    first = jnp.broadcast_to(hs_ref[...], (bb, tt, d)).reshape(m, d)
    is_first = lax.broadcasted_iota(jnp.int32, (m, d), 0) % tt == 0
    prev = jnp.where(is_first, first, pltpu.roll(h, 1, axis=0))
    last = h3[:, tt - 1:tt, :]
    hs_ref[...] = last
    sh_o[...] = last

    xx = prev - h
    mix = mix_ref[...]
    xr, xw, xk, xv, xa, xg = ((h + xx * mix[i:i + 1, :]).astype(BF16) for i in range(6))

    r = jnp.dot(xr, wr_ref[...], preferred_element_type=F32)
    k = jnp.dot(xk, wk_ref[...], preferred_element_type=F32)
    v = jnp.dot(xv, wv_ref[...], preferred_element_type=F32)
    w_pre = w0_ref[...] + _bdot(jnp.tanh(jnp.dot(xw, w1_ref[...], preferred_element_type=F32)), w2_ref[...])
    lw = -math.exp(-0.5) * jax.nn.sigmoid(w_pre)
    if has_vfirst:
        gate = jax.nn.sigmoid(v0_ref[...] + _bdot(jnp.dot(xv, v1_ref[...], preferred_element_type=F32), v2_ref[...]))
        v = v + (vf_ref[...].reshape(m, d) - v) * gate
    a = jax.nn.sigmoid(a0_ref[...] + _bdot(jnp.dot(xa, a1_ref[...], preferred_element_type=F32), a2_ref[...]))
    g = _bdot(jax.nn.sigmoid(jnp.dot(xg, g1_ref[...], preferred_element_type=F32)), g2_ref[...])
    kk = k * kk_ref[...]
    nrm = jnp.maximum(jnp.sqrt(_segsum(kk * kk, bd_ref[...], 3)), L2_EPS)
    kk = kk / nrm
    k = k * (1.0 + (a - 1.0) * ka_ref[...])

    shp = (bb, tt, d)
    r_o[...] = r.reshape(shp)
    lw_o[...] = lw.reshape(shp)
    k_o[...] = k.reshape(shp)
    v_o[...] = v.reshape(shp)
    kkn_o[...] = kk.reshape(shp)
    b_o[...] = (kk * a).reshape(shp)
    g_o[...] = g.reshape(shp).astype(g_o.dtype)


def _rwkv_proj(x, shift0, g, p, ri, v_first, bd, *, bb, tt, gate_dtype):
    b, t, d = x.shape
    vec = lambda a: a.reshape(1, d)
    tile = pl.BlockSpec((bb, tt, d), lambda i, j: (i, j, 0))
    row = pl.BlockSpec((bb, 1, d), lambda i, j: (i, 0, 0))
    args = [x, shift0.reshape(b, 1, d), vec(g), p['rw_mix'][ri],
            p['rw_w_r'][ri], p['rw_w_k'][ri], p['rw_w_v'][ri],
            vec(p['rw_w0'][ri]), p['rw_w1'][ri], p['rw_w2'][ri],
            vec(p['rw_a0'][ri]), p['rw_a1'][ri], p['rw_a2'][ri],
            p['rw_g1'][ri], p['rw_g2'][ri], vec(p['rw_k_k'][ri]), vec(p['rw_k_a'][ri]), bd]
    in_specs = [tile, row] + [_resident(a.shape) for a in args[2:]]
    has_vfirst = v_first is not None
    if has_vfirst:
        vi = ri - 1
        extra = [v_first, vec(p['rw_v0'][vi]), p['rw_v1'][vi], p['rw_v2'][vi]]
        args += extra
        in_specs += [tile] + [_resident(a.shape) for a in extra[1:]]
    big = jax.ShapeDtypeStruct((b, t, d), F32)
    gate = jax.ShapeDtypeStruct((b, t, d), gate_dtype)
    outs = pl.pallas_call(
        functools.partial(_rwkv_proj_kernel, has_vfirst=has_vfirst),
        grid=(b // bb, t // tt),
        in_specs=in_specs,
        out_specs=[tile] * 7 + [row],
        out_shape=[big] * 6 + [gate, jax.ShapeDtypeStruct((b, 1, d), F32)],
        scratch_shapes=[pltpu.VMEM((bb, 1, d), F32)],
        compiler_params=_params(2),
        name="rwkv_proj",
    )(*args)
    return outs


def _scan_kernel(r_ref, lw_ref, k_ref, v_ref, kk_ref, b_ref, gate_ref, s0_ref, tri_ref, lg_ref, lb_ref, rk_ref,
                 z_ref, so_ref, s_scr, *, tc, C, slab):
    bb, _, d = r_ref.shape
    n = RWKV_HEAD
    n_pairs = d // LANES
    t = pl.program_id(1)

    @pl.when(t == 0)
    def _():
        for i in range(bb):
            for p in range(n_pairs):
                s_scr[i, p] = jnp.concatenate([s0_ref[i, 2 * p], s0_ref[i, 2 * p + 1]], axis=-1)

    def load(ref, i):
        x = ref[i]
        if tc < C:
            x = jnp.concatenate([x, jnp.zeros((C - tc, d), F32)], axis=0)
        return x

    lane_lo = lax.broadcasted_iota(jnp.int32, (1, LANES), 1) < n
    row = lax.broadcasted_iota(jnp.int32, (2 * C, 2 * C), 0)
    col = lax.broadcasted_iota(jnp.int32, (2 * C, 2 * C), 1)
    row_hi = row >= C
    col_hi = col >= C
    tr = jnp.where(row_hi, row - C, row)
    tj = jnp.where(col_hi, col - C, col)
    diag = row_hi == col_hi
    m_l = jnp.logical_and(diag, tj < tr)
    m_ak = jnp.logical_and(jnp.logical_not(diag), tj < tr)
    m_incl = tj <= tr
    eye = (row == col).astype(F32)
    rl_row = lax.broadcasted_iota(jnp.int32, (2 * C, LANES), 0) >= C
    rl_lane = lax.broadcasted_iota(jnp.int32, (2 * C, LANES), 1) >= n
    m_rl = rl_row == rl_lane

    def stack_masked(x):
        return jnp.concatenate([jnp.where(lane_lo, x, 0.0), jnp.where(lane_lo, 0.0, x)], axis=0).astype(BF16)

    def head_sum(x):
        lo = jnp.sum(jnp.where(lane_lo, x, 0.0), axis=-1, keepdims=True)
        hi = jnp.sum(jnp.where(lane_lo, 0.0, x), axis=-1, keepdims=True)
        return jnp.where(lane_lo, lo, hi)

    units = []
    for i in range(bb):
        r, lw, k, v, kk, b = (load(ref, i) for ref in (r_ref, lw_ref, k_ref, v_ref, kk_ref, b_ref))
        cl = _split_dot_lhs(tri_ref[...], lw, 3)
        cl_end = cl[C - 1:C, :]
        e_neg = jnp.exp(-cl)
        e_end = jnp.exp(cl_end - cl)
        a_hat = -kk * jnp.exp(cl - lw)
        r_hat = r * jnp.exp(cl)
        b_hat = b * e_neg
        k_hat = k * e_neg
        b_bar = b * e_end
        k_bar = k * e_end
        w_end = jnp.exp(cl_end)
        for p in range(n_pairs):
            sl = slice(p * LANES, (p + 1) * LANES)
            units.append(dict(i=i, p=p, sl=sl, vv=v[:, sl], a_hat=a_hat[:, sl], r_hat=r_hat[:, sl],
                              b_hat=b_hat[:, sl], k_hat=k_hat[:, sl], b_bar=b_bar[:, sl], k_bar=k_bar[:, sl],
                              w_end=w_end[:, sl], rk=r[:, sl] * k[:, sl]))

    for u in units:
        bh, kh = u['b_hat'], u['k_hat']
        u['s_old'] = s_scr[u['i'], u['p']]
        rhs = jnp.concatenate([u['s_old'], u['s_old'], jnp.where(lane_lo, bh, kh), jnp.where(lane_lo, kh, bh)],
                              axis=0).astype(BF16)
        lhs = jnp.concatenate([stack_masked(u['a_hat']), stack_masked(u['r_hat'])], axis=0)
        gp = lax.dot_general(lhs, rhs, NT, preferred_element_type=F32)
        u['p_a'], g_a = gp[:2 * C, :LANES], gp[:2 * C, LANES:]
        u['p_r'], g_r = gp[2 * C:, :LANES], gp[2 * C:, LANES:]
        u['l_mat'] = jnp.where(m_l, g_a, 0.0)
        u['t_mat'] = eye + u['l_mat']
        u['ak'] = jnp.where(m_ak, g_a, 0.0)
        u['g_r'] = jnp.where(m_incl, g_r, 0.0)

    n_factors = int(math.log2(C)) - 1
    for u in units:
        u['l_pow'] = _bdot(u['l_mat'], u['l_mat'])
    for _ in range(n_factors - 1):
        for u in units:
            both = _bdot(jnp.concatenate([u['t_mat'], u['l_pow']], axis=0), u['l_pow'])
            u['t_mat'] = u['t_mat'] + both[:2 * C]
            u['l_pow'] = both[2 * C:]
    for u in units:
        u['t_mat'] = u['t_mat'] + _bdot(u['t_mat'], u['l_pow'])

    for u in units:
        vv2 = jnp.concatenate([u['vv'], u['vv']], axis=0)
        u['rhs_u'] = jnp.where(m_rl, u['p_a'] + _bdot(u['ak'], vv2), 0.0)
    for u in units:
        u_st = _bdot(u['t_mat'], u['rhs_u'])
        u['u_lo'], u['u_hi'] = u_st[:C], u_st[C:]
    for u in units:
        vv = u['vv']
        w_mix = jnp.concatenate([jnp.where(lane_lo, u['u_lo'], vv), jnp.where(lane_lo, vv, u['u_hi'])], axis=0)
        y_st = jnp.where(m_rl, u['p_r'] + _bdot(u['g_r'], w_mix), 0.0)
        y = y_st[:C] + y_st[C:]
        sl = u['sl']
        yc = y - head_sum(y) * (1.0 / n)
        yn = yc * lax.rsqrt(head_sum(yc * yc) * (1.0 / n) + LNX_EPS) * lg_ref[:, sl] + lb_ref[:, sl]
        bonus = head_sum(u['rk'] * rk_ref[:, sl]) * vv
        z = (yn[:tc] + bonus[:tc]) * gate_ref[u['i'], :, sl].astype(F32)
        z_ref[u['i'], :, sl] = z.astype(z_ref.dtype)
    for u in units:
        uv_t = jnp.concatenate([u['u_lo'] + u['u_hi'], u['vv']], axis=0).T
        bk = jnp.concatenate([u['b_bar'], u['k_bar']], axis=0)
        full = _bdot(uv_t, bk)
        s_new = u['s_old'] * u['w_end'] + jnp.where(lane_lo, full[:n], full[n:])
        s_scr[u['i'], u['p']] = s_new

    @pl.when(t == pl.num_programs(1) - 1)
    def _():
        dst = so_ref if slab is None else so_ref.at[slab]
        for i in range(bb):
            for p in range(n_pairs):
                s_fin = s_scr[i, p]
                dst[i, 2 * p] = s_fin[:, :n]
                dst[i, 2 * p + 1] = s_fin[:, n:]
        if slab is not None:
            for other in range(so_ref.shape[0]):
                if other != slab:
                    so_ref[other] = jnp.zeros(so_ref.shape[1:], F32)


def _rwkv_scan(r, lw, k, v, kk, b, gate, s_all, ri, carry, tri, lnx_g, lnx_b, r_k, *, bb, tc, out_dtype):
    bsz, t, d = r.shape
    n_pairs = d // LANES
    chunk = tri.shape[0]
    vec = lambda a: a.reshape(1, d)
    tile = pl.BlockSpec((bb, tc, d), lambda i, j: (i, j, 0))
    st = pl.BlockSpec((None, bb) + s_all.shape[2:], lambda i, j: (ri, i, 0, 0, 0))
    args = [r, lw, k, v, kk, b, gate, s_all, tri, vec(lnx_g), vec(lnx_b), vec(r_k)]
    in_specs = [tile] * 7 + [st, _resident(tri.shape)] + [_resident((1, d))] * 3
    aliases = {}
    if carry is None:
        body = functools.partial(_scan_kernel, tc=tc, C=chunk, slab=ri)
        st_out = pl.BlockSpec((s_all.shape[0], bb) + s_all.shape[2:], lambda i, j: (0, i, 0, 0, 0))
    else:
        args.append(carry)
        in_specs.append(pl.BlockSpec(memory_space=pl.ANY))
        aliases = {len(args) - 1: 1}
        body = _drop_ref(functools.partial(_scan_kernel, tc=tc, C=chunk, slab=None), len(args) - 1)
        st_out = st
    return pl.pallas_call(
        body,
        grid=(bsz // bb, t // tc),
        in_specs=in_specs,
        out_specs=[tile, st_out],
        out_shape=[jax.ShapeDtypeStruct((bsz, t, d), out_dtype), jax.ShapeDtypeStruct(s_all.shape, F32)],
        scratch_shapes=[pltpu.VMEM((bb, n_pairs, RWKV_HEAD, LANES), F32)],
        input_output_aliases=aliases,
        compiler_params=_params(2),
        name="rwkv_scan",
    )(*args)


def _trunk(x, mem_k, mem_v, conv_state, shift_state, wkv_state, p, consts, cfg, mem_heads=None):
    bsz, t, d = x.shape
    m = bsz * t
    depth = p['norm_g'].shape[0]
    bd = consts
    ar = jnp.arange(cfg['scan_chunk'])
    tri = (ar[None, :] <= ar[:, None]).astype(BF16)
    x2 = x.reshape(m, d)
    conv_new, wkv_new, shift_new = None, None, []
    v_first = None
    for layer in range(depth):
        g = p['norm_g'][layer]
        pending = None
        if layer % 2 == 0:
            ci = layer // 2
            x3, conv_new = _conv_mixer(x2.reshape(bsz, t, d), g[0], conv_state, ci, conv_new, p,
                                       bb=cfg['conv_bb'], tt=cfg['conv_tt'])
            x2 = x3.reshape(m, d)
        else:
            ri = layer // 2
            r, lw, k, v, kk, b, gt, sh = _rwkv_proj(x2.reshape(bsz, t, d), shift_state[ri], g[0], p, ri, v_first, bd,
                                                    bb=cfg['proj_bb'], tt=cfg['proj_tt'],
                                                    gate_dtype=cfg['gate_dtype'])
            if v_first is None:
                v_first = v
            z, wkv_new = _rwkv_scan(r, lw, k, v, kk, b, gt, wkv_state, ri, wkv_new, tri, p['rw_lnx_g'][ri],
                                    p['rw_lnx_b'][ri], p['rw_r_k'][ri], bb=cfg['scan_bb'], tc=cfg['scan_tc'],
                                    out_dtype=cfg['gate_dtype'])
            shift_new.append(sh.reshape(bsz, d))
            pending = (z.reshape(m, d), p['rw_w_o'][ri])
        last = layer == depth - 1
        if mem_heads is not None:
            x2 = _xattn_mlp_fused(x2, pending, g[1], p['xa_w_q'][layer], mem_heads[0], mem_heads[1], layer,
                                  p['xa_w_o'][layer], g[2], p['mlp_w_up'][layer], p['mlp_w_down'][layer],
                                  p['final_g'], seq_len=t, tt=cfg['attn_tt'], final_norm=last)
        else:
            if pending is not None:
                x2 = _matmul_res(pending[0], pending[1], x2)
            q = _norm_matmul(x2, g[1], p['xa_w_q'][layer], out_dtype=cfg['attn_dtype'])
            o = _attn(q.reshape(bsz, t, d), mem_k, mem_v, layer, bb=cfg['attn_bb'], tt=cfg['attn_tt'],
                      out_dtype=cfg['attn_dtype'])
            x2 = _matmul_res(o.reshape(m, d), p['xa_w_o'][layer], x2)
            x2 = _mlp(x2, g[2], p['mlp_w_up'][layer], p['mlp_w_down'][layer], p['final_g'], final_norm=last)
    return x2.reshape(bsz, t, d), conv_new, jnp.stack(shift_new), wkv_new


PROMPT_CFG = dict(conv_bb=1, conv_tt=256, proj_bb=1, proj_tt=256, gate_dtype=BF16, scan_bb=4, scan_tc=SCAN_CHUNK,
                  scan_chunk=SCAN_CHUNK, attn_bb=1, attn_tt=512, attn_dtype=BF16)
SAMPLE_CFG = dict(conv_bb=16, conv_tt=8, proj_bb=16, proj_tt=8, gate_dtype=F32, scan_bb=4, scan_tc=8,
                  scan_chunk=8, attn_bb=4, attn_tt=8, attn_dtype=F32)


def kernel(x_prompt, x_sample, mem_prompt, cache_mem_k, cache_mem_v, state_conv, state_shift, state_wkv, norm_g, final_g, cv_w_in, cv_b_in, cv_w_dw, cv_b_dw, cv_ln_g, cv_ln_b, cv_w_out, cv_b_out, rw_mix, rw_w_r, rw_w_k, rw_w_v, rw_w_o, rw_w0, rw_w1, rw_w2, rw_a0, rw_a1, rw_a2, rw_v0, rw_v1, rw_v2, rw_g1, rw_g2, rw_k_k, rw_k_a, rw_r_k, rw_lnx_g, rw_lnx_b, xa_w_q, xa_w_kv, xa_w_o, mlp_w_up, mlp_w_down):
    bf = lambda w: w.astype(BF16)
    p = {
        'norm_g': norm_g, 'final_g': final_g,
        'cv_w_in': bf(cv_w_in), 'cv_b_in': cv_b_in, 'cv_w_dw': cv_w_dw, 'cv_b_dw': cv_b_dw,
        'cv_ln_g': cv_ln_g, 'cv_ln_b': cv_ln_b, 'cv_w_out': bf(cv_w_out), 'cv_b_out': cv_b_out,
        'rw_mix': rw_mix, 'rw_w_r': bf(rw_w_r), 'rw_w_k': bf(rw_w_k), 'rw_w_v': bf(rw_w_v), 'rw_w_o': bf(rw_w_o),
        'rw_w0': rw_w0, 'rw_w1': bf(rw_w1), 'rw_w2': bf(rw_w2), 'rw_a0': rw_a0, 'rw_a1': bf(rw_a1),
        'rw_a2': bf(rw_a2), 'rw_v0': rw_v0, 'rw_v1': bf(rw_v1), 'rw_v2': bf(rw_v2), 'rw_g1': bf(rw_g1),
        'rw_g2': bf(rw_g2), 'rw_k_k': rw_k_k, 'rw_k_a': rw_k_a, 'rw_r_k': rw_r_k,
        'rw_lnx_g': rw_lnx_g, 'rw_lnx_b': rw_lnx_b,
        'xa_w_q': bf(xa_w_q), 'xa_w_o': bf(xa_w_o), 'mlp_w_up': bf(mlp_w_up), 'mlp_w_down': bf(mlp_w_down),
    }
    depth = norm_g.shape[0]
    bp, mlen, d = mem_prompt.shape
    n_conv, n_rwkv = state_conv.shape[0], state_wkv.shape[0]
    heads, n = state_wkv.shape[2], state_wkv.shape[3]
    hd = d // XA_HEADS

    idx = jnp.arange(SEG_TILE) // RWKV_HEAD
    bd = (idx[:, None] == idx[None, :]).astype(BF16)
    consts = bd

    mem_k_prompt, mem_v_prompt, mem_kh, mem_vh = _memkv(mem_prompt, bf(xa_w_kv))

    conv0 = jnp.zeros((n_conv, bp, CONV_STATE, d), F32)
    shift0 = jnp.zeros((n_rwkv, bp, d), F32)
    wkv0 = jnp.zeros((n_rwkv, bp, heads, n, n), F32)
    y_prompt, conv_prompt, shift_prompt, wkv_prompt = _trunk(
        x_prompt, mem_k_prompt, mem_v_prompt, conv0, shift0, wkv0, p, consts, PROMPT_CFG,
        mem_heads=(mem_kh, mem_vh))
    y_sample, conv_sample, shift_sample, wkv_sample = _trunk(
        x_sample, cache_mem_k, cache_mem_v, state_conv, state_shift, state_wkv, p, consts, SAMPLE_CFG)
    return (y_prompt, y_sample, mem_k_prompt, mem_v_prompt, conv_prompt, shift_prompt, wkv_prompt,
            conv_sample, shift_sample, wkv_sample)
```

```python
import functools
import math

import jax
import jax.numpy as jnp
from jax import lax
from jax.experimental import pallas as pl
from jax.experimental.pallas import tpu as pltpu

F32 = jnp.float32
BF16 = jnp.bfloat16

RMS_EPS = 1e-6
LN_EPS = 1e-5
L2_EPS = 1e-12
LNX_EPS = 64e-5
CONV_WIDTH = 31
CONV_STATE = CONV_WIDTH - 1
CONV_PAD = 32
RWKV_HEAD = 64
XA_HEADS = 4
SCAN_CHUNK = 64
LANES = 128
SUBLANES = 8
SEG_TILE = 256
VMEM_LIMIT = 56 * 1024 * 1024

NT = (((1,), (1,)), ((), ()))


def _params(n_axes):
    return pltpu.CompilerParams(dimension_semantics=("arbitrary",) * n_axes, vmem_limit_bytes=VMEM_LIMIT)


def _resident(shape):
    nd = len(shape)
    return pl.BlockSpec(tuple(shape), lambda *_: (0,) * nd, pipeline_mode=pl.Buffered(1))


def _drop_ref(body, pos):
    def wrapped(*refs):
        return body(*refs[:pos], *refs[pos + 1:])
    return wrapped


def _rms(x, g):
    ms = jnp.mean(x * x, axis=-1, keepdims=True)
    return x * lax.rsqrt(ms + RMS_EPS) * g


def _bdot(a, b):
    return jnp.dot(a.astype(BF16), b.astype(BF16), preferred_element_type=F32)


def _split_dot_rhs(x, w, passes):
    acc = None
    rem = x
    for i in range(passes):
        part = rem.astype(BF16)
        term = jnp.dot(part, w, preferred_element_type=F32)
        acc = term if acc is None else acc + term
        if i + 1 < passes:
            rem = rem - part.astype(F32)
    return acc


def _split_dot_lhs(w, x, passes):
    acc = None
    rem = x
    for i in range(passes):
        part = rem.astype(BF16)
        term = jnp.dot(w, part, preferred_element_type=F32)
        acc = term if acc is None else acc + term
        if i + 1 < passes:
            rem = rem - part.astype(F32)
    return acc


def _segsum(x, bd, passes):
    d = x.shape[-1]
    outs = []
    for c in range(d // SEG_TILE):
        outs.append(_split_dot_rhs(x[:, c * SEG_TILE:(c + 1) * SEG_TILE], bd, passes))
    return jnp.concatenate(outs, axis=-1)


def _norm_matmul_kernel(x_ref, g_ref, w_ref, o_ref):
    h = _rms(x_ref[...], g_ref[...]).astype(BF16)
    o_ref[...] = jnp.dot(h, w_ref[...], preferred_element_type=F32).astype(o_ref.dtype)


def _norm_matmul(x, g, w, *, out_dtype=F32, tm=512):
    m, d = x.shape
    n = w.shape[1]
    tm = min(tm, m)
    return pl.pallas_call(
        _norm_matmul_kernel,
        grid=(m // tm,),
        in_specs=[pl.BlockSpec((tm, d), lambda i: (i, 0)), _resident((1, d)), _resident((d, n))],
        out_specs=pl.BlockSpec((tm, n), lambda i: (i, 0)),
        out_shape=jax.ShapeDtypeStruct((m, n), out_dtype),
        compiler_params=_params(1),
        name="norm_matmul",
    )(x, g.reshape(1, d), w)


def _matmul_res_kernel(a_ref, w_ref, x_ref, o_ref):
    o_ref[...] = x_ref[...] + jnp.dot(a_ref[...].astype(BF16), w_ref[...], preferred_element_type=F32)


def _matmul_res(a, w, x, *, tm=512):
    m, k = a.shape
    d = w.shape[1]
    tm = min(tm, m)
    return pl.pallas_call(
        _matmul_res_kernel,
        grid=(m // tm,),
        in_specs=[pl.BlockSpec((tm, k), lambda i: (i, 0)), _resident((k, d)), pl.BlockSpec((tm, d), lambda i: (i, 0))],
        out_specs=pl.BlockSpec((tm, d), lambda i: (i, 0)),
        out_shape=jax.ShapeDtypeStruct((m, d), F32),
        compiler_params=_params(1),
        name="matmul_res",
    )(a, w, x)


MLP_CHUNKS = 4


def _mlp_block(x, g_ref, wu_ref, wd_ref, fg_ref, final_norm):
    h = _rms(x, g_ref[...]).astype(BF16)
    ck = wu_ref.shape[1] // MLP_CHUNKS
    acc = x
    for c in range(MLP_CHUNKS):
        u = jnp.dot(h, wu_ref[:, c * ck:(c + 1) * ck], preferred_element_type=F32)
        u = jnp.maximum(u, 0.0)
        u = (u * u).astype(BF16)
        acc = acc + jnp.dot(u, wd_ref[c * ck:(c + 1) * ck, :], preferred_element_type=F32)
    if final_norm:
        acc = _rms(acc, fg_ref[...])
    return acc


def _mlp_kernel(x_ref, g_ref, wu_ref, wd_ref, fg_ref, o_ref, *, final_norm):
    o_ref[...] = _mlp_block(x_ref[...], g_ref, wu_ref, wd_ref, fg_ref, final_norm)


def _mlp(x, g, w_up, w_down, final_g, *, final_norm, tm=512):
    m, d = x.shape
    dff = w_up.shape[1]
    tm = min(tm, m)
    return pl.pallas_call(
        functools.partial(_mlp_kernel, final_norm=final_norm),
        grid=(m // tm,),
        in_specs=[pl.BlockSpec((tm, d), lambda i: (i, 0)), _resident((1, d)),
                  _resident((d, dff)), _resident((dff, d)), _resident((1, d))],
        out_specs=pl.BlockSpec((tm, d), lambda i: (i, 0)),
        out_shape=jax.ShapeDtypeStruct((m, d), F32),
        compiler_params=_params(1),
        name="mlp",
    )(x, g.reshape(1, d), w_up, w_down, final_g.reshape(1, d))


def _memkv_kernel(x_ref, w_ref, k_ref, v_ref, kh_ref, vh_ref):
    r = jnp.dot(x_ref[...].astype(BF16), w_ref[0], preferred_element_type=F32)
    tb, mlen, heads, hd = k_ref.shape
    d = heads * hd
    for s in range(tb):
        rows = slice(s * mlen, (s + 1) * mlen)
        for h in range(heads):
            kb = r[rows, h * hd:(h + 1) * hd]
            vb = r[rows, d + h * hd:d + (h + 1) * hd]
            k_ref[s, :, h, :] = kb
            v_ref[s, :, h, :] = vb
            kh_ref[s, h] = kb.astype(BF16)
            vh_ref[s, h] = vb.astype(BF16)


def _memkv(mem, w_kv, *, tb=2):
    bp, mlen, d = mem.shape
    depth = w_kv.shape[0]
    hd = d // XA_HEADS
    out = jax.ShapeDtypeStruct((depth, bp, mlen, XA_HEADS, hd), F32)
    out_h = jax.ShapeDtypeStruct((depth, bp, XA_HEADS, mlen, hd), BF16)
    blk = pl.BlockSpec((None, tb, mlen, XA_HEADS, hd), lambda l, i: (l, i, 0, 0, 0))
    blk_h = pl.BlockSpec((None, tb, XA_HEADS, mlen, hd), lambda l, i: (l, i, 0, 0, 0))
    return pl.pallas_call(
        _memkv_kernel,
        grid=(depth, bp // tb),
        in_specs=[pl.BlockSpec((tb * mlen, d), lambda l, i: (i, 0)),
                  pl.BlockSpec((1, d, 2 * d), lambda l, i: (l, 0, 0))],
        out_specs=[blk, blk, blk_h, blk_h],
        out_shape=[out, out, out_h, out_h],
        compiler_params=_params(2),
        name="memkv",
    )(mem.reshape(bp * mlen, d), w_kv)


def _xattn_mlp_kernel(*refs, scale, final_norm, has_pre):
    if has_pre:
        pre_ref, wpre_ref = refs[:2]
        refs = refs[2:]
    x_ref, ga_ref, wq_ref, k_ref, v_ref, wo_ref, gm_ref, wu_ref, wd_ref, fg_ref, o_ref = refs
    heads, _, hd = k_ref.shape
    x = x_ref[...]
    if has_pre:
        x = x + jnp.dot(pre_ref[...], wpre_ref[...], preferred_element_type=F32)
    q = jnp.dot(_rms(x, ga_ref[...]).astype(BF16), wq_ref[...], preferred_element_type=F32).astype(BF16)
    outs = []
    for h in range(heads):
        s = lax.dot_general(q[:, h * hd:(h + 1) * hd], k_ref[h], NT, preferred_element_type=F32) * scale
        e = jnp.exp(s - jnp.max(s, axis=-1, keepdims=True))
        p = (e / jnp.sum(e, axis=-1, keepdims=True)).astype(BF16)
        outs.append(jnp.dot(p, v_ref[h], preferred_element_type=F32).astype(BF16))
    o = jnp.concatenate(outs, axis=-1)
    x = x + jnp.dot(o, wo_ref[...], preferred_element_type=F32)
    o_ref[...] = _mlp_block(x, gm_ref, wu_ref, wd_ref, fg_ref, final_norm)


def _xattn_mlp_fused(x2, pre, g_attn, w_q, kh, vh, layer, w_o, g_mlp, w_up, w_down, final_g, *, seq_len, tt,
                     final_norm):
    m, d = x2.shape
    _, _, heads, mlen, hd = kh.shape
    per_seq = seq_len // tt
    kv = pl.BlockSpec((None, None, heads, mlen, hd), lambda i: (layer, i // per_seq, 0, 0, 0))
    tile = pl.BlockSpec((tt, d), lambda i: (i, 0))
    vec = lambda a: a.reshape(1, d)
    args = [x2, vec(g_attn), w_q, kh, vh, w_o, vec(g_mlp), w_up, w_down, vec(final_g)]
    in_specs = [tile, _resident((1, d)), _resident(w_q.shape), kv, kv, _resident(w_o.shape),
                _resident((1, d)), _resident(w_up.shape), _resident(w_down.shape), _resident((1, d))]
    if pre is not None:
        args = [pre[0], pre[1]] + args
        in_specs = [tile, _resident(pre[1].shape)] + in_specs
    return pl.pallas_call(
        functools.partial(_xattn_mlp_kernel, scale=hd ** -0.5, final_norm=final_norm, has_pre=pre is not None),
        grid=(m // tt,),
        in_specs=in_specs,
        out_specs=tile,
        out_shape=jax.ShapeDtypeStruct((m, d), F32),
        compiler_params=_params(1),
        name="xattn_mlp_fused",
    )(*args)


def _attn_kernel(q_ref, k_ref, v_ref, o_ref, *, scale):
    bb, tt, _ = q_ref.shape
    m, heads, hd = k_ref.shape[1], k_ref.shape[2], k_ref.shape[3]
    row_head = lax.broadcasted_iota(jnp.int32, (heads * tt, m * heads), 0) // tt
    col_head = lax.broadcasted_iota(jnp.int32, (heads * tt, m * heads), 1) % heads
    own = row_head == col_head
    for i in range(bb):
        k2 = k_ref[i].reshape(m * heads, hd).astype(BF16)
        v2 = v_ref[i].reshape(m * heads, hd).astype(BF16)
        q2 = jnp.concatenate([q_ref[i, :, h * hd:(h + 1) * hd] for h in range(heads)], axis=0).astype(BF16)
        s = lax.dot_general(q2, k2, NT, preferred_element_type=F32) * scale
        s = jnp.where(own, s, -jnp.inf)
        e = jnp.exp(s - jnp.max(s, axis=-1, keepdims=True))
        p = (e / jnp.sum(e, axis=-1, keepdims=True)).astype(BF16)
        o2 = jnp.dot(p, v2, preferred_element_type=F32)
        o_ref[i] = jnp.concatenate([o2[h * tt:(h + 1) * tt] for h in range(heads)], axis=-1).astype(o_ref.dtype)


def _attn(q, mem_k, mem_v, layer, *, bb, tt, out_dtype):
    b, t, d = q.shape
    _, _, m, heads, hd = mem_k.shape
    kv = pl.BlockSpec((None, bb, m, heads, hd), lambda i, j: (layer, i, 0, 0, 0))
    return pl.pallas_call(
        functools.partial(_attn_kernel, scale=hd ** -0.5),
        grid=(b // bb, t // tt),
        in_specs=[pl.BlockSpec((bb, tt, d), lambda i, j: (i, j, 0)), kv, kv],
        out_specs=pl.BlockSpec((bb, tt, d), lambda i, j: (i, j, 0)),
        out_shape=jax.ShapeDtypeStruct((b, t, d), out_dtype),
        compiler_params=_params(2),
        name="attn_core",
    )(q, mem_k, mem_v)


def _conv_kernel(x_ref, c0_ref, g_ref, win_ref, bin_ref, f_ref, bdw_ref, lg_ref, lb_ref, wout_ref, bout_ref,
                 o_ref, cs_ref, up_ref, *, lane_chunk, slab):
    bb, tt, d = x_ref.shape
    t = pl.program_id(1)
    lo = CONV_PAD - CONV_STATE
    x = x_ref[...].reshape(bb * tt, d)
    pre = jnp.dot(_rms(x, g_ref[...]).astype(BF16), win_ref[...], preferred_element_type=F32) + bin_ref[...]
    u = (pre[:, :d] * jax.nn.sigmoid(pre[:, d:])).reshape(bb, tt, d)

    @pl.when(t == 0)
    def _():
        up_ref[:, 0:lo, :] = jnp.zeros((bb, lo, d), F32)
        up_ref[:, CONV_PAD + tt:, :] = jnp.zeros((bb, SUBLANES, d), F32)
        up_ref[:, lo:CONV_PAD, :] = c0_ref[...]

    up_ref[:, CONV_PAD:CONV_PAD + tt, :] = u
    acts = []
    for i in range(bb):
        zs = []
        for c in range(d // lane_chunk):
            cs = slice(c * lane_chunk, (c + 1) * lane_chunk)
            acc = None
            for o in range(SUBLANES):
                g = None
                for j in range(CONV_WIDTH):
                    if (lo + j) % SUBLANES != o:
                        continue
                    base = lo + j - o
                    term = up_ref[i, base:base + tt + SUBLANES, cs] * f_ref[j:j + 1, cs]
                    g = term if g is None else g + term
                shifted = g[o:o + tt]
                acc = shifted if acc is None else acc + shifted
            zs.append(acc)
        z = jnp.concatenate(zs, axis=-1) + bdw_ref[...]
        zc = z - jnp.mean(z, axis=-1, keepdims=True)
        zn = zc * lax.rsqrt(jnp.mean(zc * zc, axis=-1, keepdims=True) + LN_EPS)
        zn = zn * lg_ref[...] + lb_ref[...]
        acts.append((zn * jax.nn.sigmoid(zn)).astype(BF16))
    act = acts[0] if bb == 1 else jnp.concatenate(acts, axis=0)
    out = x + jnp.dot(act, wout_ref[...], preferred_element_type=F32) + bout_ref[...]
    o_ref[...] = out.reshape(bb, tt, d)
    tail = up_ref[:, tt + lo:tt + CONV_PAD, :]
    if slab is None:
        cs_ref[...] = tail
    else:
        for other in range(cs_ref.shape[0]):
            cs_ref[other] = tail if other == slab else jnp.zeros_like(tail)
    up_ref[:, lo:CONV_PAD, :] = tail


def _conv_mixer(x, g, conv_all, ci, carry, p, *, bb, tt):
    b, t, d = x.shape
    vec = lambda a: a.reshape(1, -1)
    slab = pl.BlockSpec((None, bb, CONV_STATE, d), lambda i, j: (ci, i, 0, 0))
    args = [x, conv_all, vec(g), p['cv_w_in'][ci], vec(p['cv_b_in'][ci]), p['cv_w_dw'][ci], vec(p['cv_b_dw'][ci]),
            vec(p['cv_ln_g'][ci]), vec(p['cv_ln_b'][ci]), p['cv_w_out'][ci], vec(p['cv_b_out'][ci])]
    in_specs = [pl.BlockSpec((bb, tt, d), lambda i, j: (i, j, 0)), slab] + [_resident(a.shape) for a in args[2:]]
    aliases = {}
    if carry is None:
        body = functools.partial(_conv_kernel, lane_chunk=2 * LANES, slab=ci)
        slab_out = pl.BlockSpec((conv_all.shape[0], bb, CONV_STATE, d), lambda i, j: (0, i, 0, 0))
    else:
        args.append(carry)
        in_specs.append(pl.BlockSpec(memory_space=pl.ANY))
        aliases = {len(args) - 1: 1}
        body = _drop_ref(functools.partial(_conv_kernel, lane_chunk=2 * LANES, slab=None), len(args) - 1)
        slab_out = slab
    return pl.pallas_call(
        body,
        grid=(b // bb, t // tt),
        in_specs=in_specs,
        out_specs=[pl.BlockSpec((bb, tt, d), lambda i, j: (i, j, 0)), slab_out],
        out_shape=[jax.ShapeDtypeStruct((b, t, d), F32), jax.ShapeDtypeStruct(conv_all.shape, F32)],
        scratch_shapes=[pltpu.VMEM((bb, CONV_PAD + tt + SUBLANES, d), F32)],
        input_output_aliases=aliases,
        compiler_params=_params(2),
        name="conv_core",
    )(*args)


def _rwkv_proj_kernel(*refs, has_vfirst):
    (x_ref, sh0_ref, g_ref, mix_ref, wr_ref, wk_ref, wv_ref, w0_ref, w1_ref, w2_ref,
     a0_ref, a1_ref, a2_ref, g1_ref, g2_ref, kk_ref, ka_ref, bd_ref) = refs[:18]
    pos = 18
    if has_vfirst:
        vf_ref, v0_ref, v1_ref, v2_ref = refs[pos:pos + 4]
        pos += 4
    (r_o, lw_o, k_o, v_o, kkn_o, b_o, g_o, sh_o) = refs[pos:pos + 8]
    hs_ref = refs[pos + 8]

    bb, tt, d = x_ref.shape
    t = pl.program_id(1)
    h3 = _rms(x_ref[...], g_ref[...])

    @pl.when(t == 0)
    def _():
        hs_ref[...] = sh0_ref[...]

    m = bb * tt
    h = h3.reshape(m, d)
    first = jnp.broadcast_to(hs_ref[...], (bb, tt, d)).reshape(m, d)
    is_first = lax.broadcasted_iota(jnp.int32, (m, d), 0) % tt == 0
    prev = jnp.where(is_first, first, pltpu.roll(h, 1, axis=0))
    last = h3[:, tt - 1:tt, :]
    hs_ref[...] = last
    sh_o[...] = last

    xx = prev - h
    mix = mix_ref[...]
    xr, xw, xk, xv, xa, xg = ((h + xx * mix[i:i + 1, :]).astype(BF16) for i in range(6))

    r = jnp.dot(xr, wr_ref[...], preferred_element_type=F32)
    k = jnp.dot(xk, wk_ref[...], preferred_element_type=F32)
    v = jnp.dot(xv, wv_ref[...], preferred_element_type=F32)
    w_pre = w0_ref[...] + _bdot(jnp.tanh(jnp.dot(xw, w1_ref[...], preferred_element_type=F32)), w2_ref[...])
    lw = -math.exp(-0.5) * jax.nn.sigmoid(w_pre)
    if has_vfirst:
        gate = jax.nn.sigmoid(v0_ref[...] + _bdot(jnp.dot(xv, v1_ref[...], preferred_element_type=F32), v2_ref[...]))
        v = v + (vf_ref[...].reshape(m, d) - v) * gate
    a = jax.nn.sigmoid(a0_ref[...] + _bdot(jnp.dot(xa, a1_ref[...], preferred_element_type=F32), a2_ref[...]))
    g = _bdot(jax.nn.sigmoid(jnp.dot(xg, g1_ref[...], preferred_element_type=F32)), g2_ref[...])
    kk = k * kk_ref[...]
    nrm = jnp.maximum(jnp.sqrt(_segsum(kk * kk, bd_ref[...], 3)), L2_EPS)
    kk = kk / nrm
    k = k * (1.0 + (a - 1.0) * ka_ref[...])

    shp = (bb, tt, d)
    r_o[...] = r.reshape(shp)
    lw_o[...] = lw.reshape(shp)
    k_o[...] = k.reshape(shp)
    v_o[...] = v.reshape(shp)
    kkn_o[...] = kk.reshape(shp)
    b_o[...] = (kk * a).reshape(shp)
    g_o[...] = g.reshape(shp).astype(g_o.dtype)


def _rwkv_proj(x, shift0, g, p, ri, v_first, bd, *, bb, tt, gate_dtype):
    b, t, d = x.shape
    vec = lambda a: a.reshape(1, d)
    tile = pl.BlockSpec((bb, tt, d), lambda i, j: (i, j, 0))
    row = pl.BlockSpec((bb, 1, d), lambda i, j: (i, 0, 0))
    args = [x, shift0.reshape(b, 1, d), vec(g), p['rw_mix'][ri],
            p['rw_w_r'][ri], p['rw_w_k'][ri], p['rw_w_v'][ri],
            vec(p['rw_w0'][ri]), p['rw_w1'][ri], p['rw_w2'][ri],
            vec(p['rw_a0'][ri]), p['rw_a1'][ri], p['rw_a2'][ri],
            p['rw_g1'][ri], p['rw_g2'][ri], vec(p['rw_k_k'][ri]), vec(p['rw_k_a'][ri]), bd]
    in_specs = [tile, row] + [_resident(a.shape) for a in args[2:]]
    has_vfirst = v_first is not None
    if has_vfirst:
        vi = ri - 1
        extra = [v_first, vec(p['rw_v0'][vi]), p['rw_v1'][vi], p['rw_v2'][vi]]
        args += extra
        in_specs += [tile] + [_resident(a.shape) for a in extra[1:]]
    big = jax.ShapeDtypeStruct((b, t, d), F32)
    gate = jax.ShapeDtypeStruct((b, t, d), gate_dtype)
    outs = pl.pallas_call(
        functools.partial(_rwkv_proj_kernel, has_vfirst=has_vfirst),
        grid=(b // bb, t // tt),
        in_specs=in_specs,
        out_specs=[tile] * 7 + [row],
        out_shape=[big] * 6 + [gate, jax.ShapeDtypeStruct((b, 1, d), F32)],
        scratch_shapes=[pltpu.VMEM((bb, 1, d), F32)],
        compiler_params=_params(2),
        name="rwkv_proj",
    )(*args)
    return outs


def _scan_kernel(r_ref, lw_ref, k_ref, v_ref, kk_ref, b_ref, gate_ref, s0_ref, tri_ref, lg_ref, lb_ref, rk_ref,
                 z_ref, so_ref, s_scr, *, tc, C, slab):
    bb, _, d = r_ref.shape
    n = RWKV_HEAD
    n_pairs = d // LANES
    t = pl.program_id(1)

    @pl.when(t == 0)
    def _():
        for i in range(bb):
            for p in range(n_pairs):
                s_scr[i, p] = jnp.concatenate([s0_ref[i, 2 * p], s0_ref[i, 2 * p + 1]], axis=-1)

    def load(ref, i):
        x = ref[i]
        if tc < C:
            x = jnp.concatenate([x, jnp.zeros((C - tc, d), F32)], axis=0)
        return x

    lane_lo = lax.broadcasted_iota(jnp.int32, (1, LANES), 1) < n
    row = lax.broadcasted_iota(jnp.int32, (2 * C, 2 * C), 0)
    col = lax.broadcasted_iota(jnp.int32, (2 * C, 2 * C), 1)
    row_hi = row >= C
    col_hi = col >= C
    tr = jnp.where(row_hi, row - C, row)
    tj = jnp.where(col_hi, col - C, col)
    diag = row_hi == col_hi
    m_l = jnp.logical_and(diag, tj < tr)
    m_ak = jnp.logical_and(jnp.logical_not(diag), tj < tr)
    m_incl = tj <= tr
    eye = (row == col).astype(F32)
    rl_row = lax.broadcasted_iota(jnp.int32, (2 * C, LANES), 0) >= C
    rl_lane = lax.broadcasted_iota(jnp.int32, (2 * C, LANES), 1) >= n
    m_rl = rl_row == rl_lane

    def stack_masked(x):
        return jnp.concatenate([jnp.where(lane_lo, x, 0.0), jnp.where(lane_lo, 0.0, x)], axis=0).astype(BF16)

    def head_sum(x):
        lo = jnp.sum(jnp.where(lane_lo, x, 0.0), axis=-1, keepdims=True)
        hi = jnp.sum(jnp.where(lane_lo, 0.0, x), axis=-1, keepdims=True)
        return jnp.where(lane_lo, lo, hi)

    units = []
    for i in range(bb):
        r, lw, k, v, kk, b = (load(ref, i) for ref in (r_ref, lw_ref, k_ref, v_ref, kk_ref, b_ref))
        cl = _split_dot_lhs(tri_ref[...], lw, 3)
        cl_end = cl[C - 1:C, :]
        e_neg = jnp.exp(-cl)
        e_end = jnp.exp(cl_end - cl)
        a_hat = -kk * jnp.exp(cl - lw)
        r_hat = r * jnp.exp(cl)
        b_hat = b * e_neg
        k_hat = k * e_neg
        b_bar = b * e_end
        k_bar = k * e_end
        w_end = jnp.exp(cl_end)
        for p in range(n_pairs):
            sl = slice(p * LANES, (p + 1) * LANES)
            units.append(dict(i=i, p=p, sl=sl, vv=v[:, sl], a_hat=a_hat[:, sl], r_hat=r_hat[:, sl],
                              b_hat=b_hat[:, sl], k_hat=k_hat[:, sl], b_bar=b_bar[:, sl], k_bar=k_bar[:, sl],
                              w_end=w_end[:, sl], rk=r[:, sl] * k[:, sl]))

    for u in units:
        bh, kh = u['b_hat'], u['k_hat']
        u['s_old'] = s_scr[u['i'], u['p']]
        rhs = jnp.concatenate([u['s_old'], u['s_old'], jnp.where(lane_lo, bh, kh), jnp.where(lane_lo, kh, bh)],
                              axis=0).astype(BF16)
        lhs = jnp.concatenate([stack_masked(u['a_hat']), stack_masked(u['r_hat'])], axis=0)
        gp = lax.dot_general(lhs, rhs, NT, preferred_element_type=F32)
        u['p_a'], g_a = gp[:2 * C, :LANES], gp[:2 * C, LANES:]
        u['p_r'], g_r = gp[2 * C:, :LANES], gp[2 * C:, LANES:]
        l_mat = jnp.where(m_l, g_a, 0.0)
        u['t_mat'] = eye + l_mat
        u['l_pow'] = l_mat.astype(BF16)
        u['ak'] = jnp.where(m_ak, g_a, 0.0).astype(BF16)
        u['g_r'] = jnp.where(m_incl, g_r, 0.0).astype(BF16)

    n_factors = int(math.log2(C)) - 1
    for u in units:
        u['l_pow'] = _bdot(u['l_pow'], u['l_pow']).astype(BF16)
    for _ in range(n_factors - 1):
        for u in units:
            both = _bdot(jnp.concatenate([u['t_mat'].astype(BF16), u['l_pow']], axis=0), u['l_pow'])
            u['t_mat'] = u['t_mat'] + both[:2 * C]
            u['l_pow'] = both[2 * C:].astype(BF16)
    for u in units:
        u['t_mat'] = u['t_mat'] + _bdot(u['t_mat'], u['l_pow'])

    for u in units:
        vv2 = jnp.concatenate([u['vv'], u['vv']], axis=0)
        u['rhs_u'] = jnp.where(m_rl, u['p_a'] + _bdot(u['ak'], vv2), 0.0)
    for u in units:
        u_st = _bdot(u['t_mat'], u['rhs_u'])
        u['u_lo'], u['u_hi'] = u_st[:C], u_st[C:]
    for u in units:
        vv = u['vv']
        w_mix = jnp.concatenate([jnp.where(lane_lo, u['u_lo'], vv), jnp.where(lane_lo, vv, u['u_hi'])], axis=0)
        y_st = jnp.where(m_rl, u['p_r'] + _bdot(u['g_r'], w_mix), 0.0)
        y = y_st[:C] + y_st[C:]
        sl = u['sl']
        yc = y - head_sum(y) * (1.0 / n)
        yn = yc * lax.rsqrt(head_sum(yc * yc) * (1.0 / n) + LNX_EPS) * lg_ref[:, sl] + lb_ref[:, sl]
        bonus = head_sum(u['rk'] * rk_ref[:, sl]) * vv
        z = (yn[:tc] + bonus[:tc]) * gate_ref[u['i'], :, sl].astype(F32)
        z_ref[u['i'], :, sl] = z.astype(z_ref.dtype)
    for u in units:
        uv_t = jnp.concatenate([u['u_lo'] + u['u_hi'], u['vv']], axis=0).T
        bk = jnp.concatenate([u['b_bar'], u['k_bar']], axis=0)
        full = _bdot(uv_t, bk)
        s_new = u['s_old'] * u['w_end'] + jnp.where(lane_lo, full[:n], full[n:])
        s_scr[u['i'], u['p']] = s_new

    @pl.when(t == pl.num_programs(1) - 1)
    def _():
        dst = so_ref if slab is None else so_ref.at[slab]
        for i in range(bb):
            for p in range(n_pairs):
                s_fin = s_scr[i, p]
                dst[i, 2 * p] = s_fin[:, :n]
                dst[i, 2 * p + 1] = s_fin[:, n:]
        if slab is not None:
            for other in range(so_ref.shape[0]):
                if other != slab:
                    so_ref[other] = jnp.zeros(so_ref.shape[1:], F32)


def _rwkv_scan(r, lw, k, v, kk, b, gate, s_all, ri, carry, tri, lnx_g, lnx_b, r_k, *, bb, tc, out_dtype):
    bsz, t, d = r.shape
    n_pairs = d // LANES
    chunk = tri.shape[0]
    vec = lambda a: a.reshape(1, d)
    tile = pl.BlockSpec((bb, tc, d), lambda i, j: (i, j, 0))
    st = pl.BlockSpec((None, bb) + s_all.shape[2:], lambda i, j: (ri, i, 0, 0, 0))
    args = [r, lw, k, v, kk, b, gate, s_all, tri, vec(lnx_g), vec(lnx_b), vec(r_k)]
    in_specs = [tile] * 7 + [st, _resident(tri.shape)] + [_resident((1, d))] * 3
    aliases = {}
    if carry is None:
        body = functools.partial(_scan_kernel, tc=tc, C=chunk, slab=ri)
        st_out = pl.BlockSpec((s_all.shape[0], bb) + s_all.shape[2:], lambda i, j: (0, i, 0, 0, 0))
    else:
        args.append(carry)
        in_specs.append(pl.BlockSpec(memory_space=pl.ANY))
        aliases = {len(args) - 1: 1}
        body = _drop_ref(functools.partial(_scan_kernel, tc=tc, C=chunk, slab=None), len(args) - 1)
        st_out = st
    return pl.pallas_call(
        body,
        grid=(bsz // bb, t // tc),
        in_specs=in_specs,
        out_specs=[tile, st_out],
        out_shape=[jax.ShapeDtypeStruct((bsz, t, d), out_dtype), jax.ShapeDtypeStruct(s_all.shape, F32)],
        scratch_shapes=[pltpu.VMEM((bb, n_pairs, RWKV_HEAD, LANES), F32)],
        input_output_aliases=aliases,
        compiler_params=_params(2),
        name="rwkv_scan",
    )(*args)


def _trunk(x, mem_k, mem_v, conv_state, shift_state, wkv_state, p, consts, cfg, mem_heads=None):
    bsz, t, d = x.shape
    m = bsz * t
    depth = p['norm_g'].shape[0]
    bd = consts
    ar = jnp.arange(cfg['scan_chunk'])
    tri = (ar[None, :] <= ar[:, None]).astype(BF16)
    x2 = x.reshape(m, d)
    conv_new, wkv_new, shift_new = None, None, []
    v_first = None
    for layer in range(depth):
        g = p['norm_g'][layer]
        pending = None
        if layer % 2 == 0:
            ci = layer // 2
            x3, conv_new = _conv_mixer(x2.reshape(bsz, t, d), g[0], conv_state, ci, conv_new, p,
                                       bb=cfg['conv_bb'], tt=cfg['conv_tt'])
            x2 = x3.reshape(m, d)
        else:
            ri = layer // 2
            r, lw, k, v, kk, b, gt, sh = _rwkv_proj(x2.reshape(bsz, t, d), shift_state[ri], g[0], p, ri, v_first, bd,
                                                    bb=cfg['proj_bb'], tt=cfg['proj_tt'],
                                                    gate_dtype=cfg['gate_dtype'])
            if v_first is None:
                v_first = v
            z, wkv_new = _rwkv_scan(r, lw, k, v, kk, b, gt, wkv_state, ri, wkv_new, tri, p['rw_lnx_g'][ri],
                                    p['rw_lnx_b'][ri], p['rw_r_k'][ri], bb=cfg['scan_bb'], tc=cfg['scan_tc'],
                                    out_dtype=cfg['gate_dtype'])
            shift_new.append(sh.reshape(bsz, d))
            pending = (z.reshape(m, d), p['rw_w_o'][ri])
        last = layer == depth - 1
        if mem_heads is not None:
            x2 = _xattn_mlp_fused(x2, pending, g[1], p['xa_w_q'][layer], mem_heads[0], mem_heads[1], layer,
                                  p['xa_w_o'][layer], g[2], p['mlp_w_up'][layer], p['mlp_w_down'][layer],
                                  p['final_g'], seq_len=t, tt=cfg['attn_tt'], final_norm=last)
        else:
            if pending is not None:
                x2 = _matmul_res(pending[0], pending[1], x2)
            q = _norm_matmul(x2, g[1], p['xa_w_q'][layer], out_dtype=cfg['attn_dtype'])
            o = _attn(q.reshape(bsz, t, d), mem_k, mem_v, layer, bb=cfg['attn_bb'], tt=cfg['attn_tt'],
                      out_dtype=cfg['attn_dtype'])
            x2 = _matmul_res(o.reshape(m, d), p['xa_w_o'][layer], x2)
            x2 = _mlp(x2, g[2], p['mlp_w_up'][layer], p['mlp_w_down'][layer], p['final_g'], final_norm=last)
    return x2.reshape(bsz, t, d), conv_new, jnp.stack(shift_new), wkv_new


PROMPT_CFG = dict(conv_bb=1, conv_tt=256, proj_bb=1, proj_tt=256, gate_dtype=BF16, scan_bb=2, scan_tc=SCAN_CHUNK,
                  scan_chunk=SCAN_CHUNK, attn_bb=1, attn_tt=512, attn_dtype=BF16)
SAMPLE_CFG = dict(conv_bb=16, conv_tt=8, proj_bb=16, proj_tt=8, gate_dtype=F32, scan_bb=8, scan_tc=8,
                  scan_chunk=8, attn_bb=8, attn_tt=8, attn_dtype=F32)


def kernel(x_prompt, x_sample, mem_prompt, cache_mem_k, cache_mem_v, state_conv, state_shift, state_wkv, norm_g, final_g, cv_w_in, cv_b_in, cv_w_dw, cv_b_dw, cv_ln_g, cv_ln_b, cv_w_out, cv_b_out, rw_mix, rw_w_r, rw_w_k, rw_w_v, rw_w_o, rw_w0, rw_w1, rw_w2, rw_a0, rw_a1, rw_a2, rw_v0, rw_v1, rw_v2, rw_g1, rw_g2, rw_k_k, rw_k_a, rw_r_k, rw_lnx_g, rw_lnx_b, xa_w_q, xa_w_kv, xa_w_o, mlp_w_up, mlp_w_down):
    bf = lambda w: w.astype(BF16)
    p = {
        'norm_g': norm_g, 'final_g': final_g,
        'cv_w_in': bf(cv_w_in), 'cv_b_in': cv_b_in, 'cv_w_dw': cv_w_dw, 'cv_b_dw': cv_b_dw,
        'cv_ln_g': cv_ln_g, 'cv_ln_b': cv_ln_b, 'cv_w_out': bf(cv_w_out), 'cv_b_out': cv_b_out,
        'rw_mix': rw_mix, 'rw_w_r': bf(rw_w_r), 'rw_w_k': bf(rw_w_k), 'rw_w_v': bf(rw_w_v), 'rw_w_o': bf(rw_w_o),
        'rw_w0': rw_w0, 'rw_w1': bf(rw_w1), 'rw_w2': bf(rw_w2), 'rw_a0': rw_a0, 'rw_a1': bf(rw_a1),
        'rw_a2': bf(rw_a2), 'rw_v0': rw_v0, 'rw_v1': bf(rw_v1), 'rw_v2': bf(rw_v2), 'rw_g1': bf(rw_g1),
        'rw_g2': bf(rw_g2), 'rw_k_k': rw_k_k, 'rw_k_a': rw_k_a, 'rw_r_k': rw_r_k,
        'rw_lnx_g': rw_lnx_g, 'rw_lnx_b': rw_lnx_b,
        'xa_w_q': bf(xa_w_q), 'xa_w_o': bf(xa_w_o), 'mlp_w_up': bf(mlp_w_up), 'mlp_w_down': bf(mlp_w_down),
    }
    depth = norm_g.shape[0]
    bp, mlen, d = mem_prompt.shape
    n_conv, n_rwkv = state_conv.shape[0], state_wkv.shape[0]
    heads, n = state_wkv.shape[2], state_wkv.shape[3]
    hd = d // XA_HEADS

    idx = jnp.arange(SEG_TILE) // RWKV_HEAD
    bd = (idx[:, None] == idx[None, :]).astype(BF16)
    consts = bd

    mem_k_prompt, mem_v_prompt, mem_kh, mem_vh = _memkv(mem_prompt, bf(xa_w_kv))

    conv0 = jnp.zeros((n_conv, bp, CONV_STATE, d), F32)
    shift0 = jnp.zeros((n_rwkv, bp, d), F32)
    wkv0 = jnp.zeros((n_rwkv, bp, heads, n, n), F32)
    y_prompt, conv_prompt, shift_prompt, wkv_prompt = _trunk(
        x_prompt, mem_k_prompt, mem_v_prompt, conv0, shift0, wkv0, p, consts, PROMPT_CFG,
        mem_heads=(mem_kh, mem_vh))
    y_sample, conv_sample, shift_sample, wkv_sample = _trunk(
        x_sample, cache_mem_k, cache_mem_v, state_conv, state_shift, state_wkv, p, consts, SAMPLE_CFG)
    return (y_prompt, y_sample, mem_k_prompt, mem_v_prompt, conv_prompt, shift_prompt, wkv_prompt,
            conv_sample, shift_sample, wkv_sample)
```

```python
import functools
import math

import jax
import jax.numpy as jnp
from jax import lax
from jax.experimental import pallas as pl
from jax.experimental.pallas import tpu as pltpu

F32 = jnp.float32
BF16 = jnp.bfloat16

RMS_EPS = 1e-6
LN_EPS = 1e-5
L2_EPS = 1e-12
LNX_EPS = 64e-5
CONV_WIDTH = 31
CONV_STATE = CONV_WIDTH - 1
CONV_PAD = 32
RWKV_HEAD = 64
XA_HEADS = 4
SCAN_CHUNK = 64
LANES = 128
SUBLANES = 8
SEG_TILE = 256
VMEM_LIMIT = 56 * 1024 * 1024

NT = (((1,), (1,)), ((), ()))


def _params(n_axes):
    return pltpu.CompilerParams(dimension_semantics=("arbitrary",) * n_axes, vmem_limit_bytes=VMEM_LIMIT)


def _resident(shape):
    nd = len(shape)
    return pl.BlockSpec(tuple(shape), lambda *_: (0,) * nd, pipeline_mode=pl.Buffered(1))


def _resident_layer(stacked_shape, layer):
    nd = len(stacked_shape) - 1
    return pl.BlockSpec((None,) + tuple(stacked_shape[1:]), lambda *_: (layer,) + (0,) * nd,
                        pipeline_mode=pl.Buffered(1))


def _drop_ref(body, pos):
    def wrapped(*refs):
        return body(*refs[:pos], *refs[pos + 1:])
    return wrapped


def _rms(x, g):
    ms = jnp.mean(x * x, axis=-1, keepdims=True)
    return x * lax.rsqrt(ms + RMS_EPS) * g


def _bdot(a, b):
    return jnp.dot(a.astype(BF16), b.astype(BF16), preferred_element_type=F32)


def _split_dot_rhs(x, w, passes):
    acc = None
    rem = x
    for i in range(passes):
        part = rem.astype(BF16)
        term = jnp.dot(part, w, preferred_element_type=F32)
        acc = term if acc is None else acc + term
        if i + 1 < passes:
            rem = rem - part.astype(F32)
    return acc


def _split_dot_lhs(w, x, passes):
    acc = None
    rem = x
    for i in range(passes):
        part = rem.astype(BF16)
        term = jnp.dot(w, part, preferred_element_type=F32)
        acc = term if acc is None else acc + term
        if i + 1 < passes:
            rem = rem - part.astype(F32)
    return acc


def _segsum(x, bd, passes):
    d = x.shape[-1]
    outs = []
    for c in range(d // SEG_TILE):
        outs.append(_split_dot_rhs(x[:, c * SEG_TILE:(c + 1) * SEG_TILE], bd, passes))
    return jnp.concatenate(outs, axis=-1)


def _norm_matmul_kernel(x_ref, g_ref, w_ref, o_ref):
    h = _rms(x_ref[...], g_ref[...]).astype(BF16)
    o_ref[...] = jnp.dot(h, w_ref[...], preferred_element_type=F32).astype(o_ref.dtype)


def _norm_matmul(x, g, w, *, out_dtype=F32, tm=512):
    m, d = x.shape
    n = w.shape[1]
    tm = min(tm, m)
    return pl.pallas_call(
        _norm_matmul_kernel,
        grid=(m // tm,),
        in_specs=[pl.BlockSpec((tm, d), lambda i: (i, 0)), _resident((1, d)), _resident((d, n))],
        out_specs=pl.BlockSpec((tm, n), lambda i: (i, 0)),
        out_shape=jax.ShapeDtypeStruct((m, n), out_dtype),
        compiler_params=_params(1),
        name="norm_matmul",
    )(x, g.reshape(1, d), w)


def _matmul_res_kernel(a_ref, w_ref, x_ref, o_ref):
    o_ref[...] = x_ref[...] + jnp.dot(a_ref[...].astype(BF16), w_ref[...], preferred_element_type=F32)


def _matmul_res(a, w, x, *, tm=512):
    m, k = a.shape
    d = w.shape[1]
    tm = min(tm, m)
    return pl.pallas_call(
        _matmul_res_kernel,
        grid=(m // tm,),
        in_specs=[pl.BlockSpec((tm, k), lambda i: (i, 0)), _resident((k, d)), pl.BlockSpec((tm, d), lambda i: (i, 0))],
        out_specs=pl.BlockSpec((tm, d), lambda i: (i, 0)),
        out_shape=jax.ShapeDtypeStruct((m, d), F32),
        compiler_params=_params(1),
        name="matmul_res",
    )(a, w, x)


MLP_CHUNKS = 4


def _mlp_block(x, g_ref, wu_ref, wd_ref, fg_ref, final_norm):
    h = _rms(x, g_ref[...]).astype(BF16)
    ck = wu_ref.shape[1] // MLP_CHUNKS
    acc = x
    for c in range(MLP_CHUNKS):
        u = jnp.dot(h, wu_ref[:, c * ck:(c + 1) * ck], preferred_element_type=F32)
        u = jnp.maximum(u, 0.0)
        u = (u * u).astype(BF16)
        acc = acc + jnp.dot(u, wd_ref[c * ck:(c + 1) * ck, :], preferred_element_type=F32)
    if final_norm:
        acc = _rms(acc, fg_ref[...])
    return acc


def _mlp_kernel(x_ref, g_ref, wu_ref, wd_ref, fg_ref, o_ref, *, final_norm):
    o_ref[...] = _mlp_block(x_ref[...], g_ref, wu_ref, wd_ref, fg_ref, final_norm)


def _mlp(x, g, w_up, w_down, layer, final_g, *, final_norm, tm=512):
    m, d = x.shape
    tm = min(tm, m)
    return pl.pallas_call(
        functools.partial(_mlp_kernel, final_norm=final_norm),
        grid=(m // tm,),
        in_specs=[pl.BlockSpec((tm, d), lambda i: (i, 0)), _resident((1, d)),
                  _resident_layer(w_up.shape, layer), _resident_layer(w_down.shape, layer), _resident((1, d))],
        out_specs=pl.BlockSpec((tm, d), lambda i: (i, 0)),
        out_shape=jax.ShapeDtypeStruct((m, d), F32),
        compiler_params=_params(1),
        name="mlp",
    )(x, g.reshape(1, d), w_up, w_down, final_g.reshape(1, d))


def _memkv_kernel(x_ref, w_ref, k_ref, v_ref, kh_ref, vh_ref):
    r = jnp.dot(x_ref[...].astype(BF16), w_ref[0], preferred_element_type=F32)
    tb, mlen, heads, hd = k_ref.shape
    d = heads * hd
    for s in range(tb):
        rows = slice(s * mlen, (s + 1) * mlen)
        for h in range(heads):
            kb = r[rows, h * hd:(h + 1) * hd]
            vb = r[rows, d + h * hd:d + (h + 1) * hd]
            k_ref[s, :, h, :] = kb
            v_ref[s, :, h, :] = vb
            kh_ref[s, h] = kb.astype(BF16)
            vh_ref[s, h] = vb.astype(BF16)


def _memkv(mem, w_kv, *, tb=2):
    bp, mlen, d = mem.shape
    depth = w_kv.shape[0]
    hd = d // XA_HEADS
    out = jax.ShapeDtypeStruct((depth, bp, mlen, XA_HEADS, hd), F32)
    out_h = jax.ShapeDtypeStruct((depth, bp, XA_HEADS, mlen, hd), BF16)
    blk = pl.BlockSpec((None, tb, mlen, XA_HEADS, hd), lambda l, i: (l, i, 0, 0, 0))
    blk_h = pl.BlockSpec((None, tb, XA_HEADS, mlen, hd), lambda l, i: (l, i, 0, 0, 0))
    return pl.pallas_call(
        _memkv_kernel,
        grid=(depth, bp // tb),
        in_specs=[pl.BlockSpec((tb * mlen, d), lambda l, i: (i, 0)),
                  pl.BlockSpec((1, d, 2 * d), lambda l, i: (l, 0, 0))],
        out_specs=[blk, blk, blk_h, blk_h],
        out_shape=[out, out, out_h, out_h],
        compiler_params=_params(2),
        name="memkv",
    )(mem.reshape(bp * mlen, d), w_kv)


def _xattn_mlp_kernel(*refs, scale, final_norm, has_pre):
    if has_pre:
        pre_ref, wpre_ref = refs[:2]
        refs = refs[2:]
    x_ref, ga_ref, wq_ref, k_ref, v_ref, wo_ref, gm_ref, wu_ref, wd_ref, fg_ref, o_ref = refs
    heads, _, hd = k_ref.shape
    x = x_ref[...]
    if has_pre:
        x = x + jnp.dot(pre_ref[...], wpre_ref[...], preferred_element_type=F32)
    q = jnp.dot(_rms(x, ga_ref[...]).astype(BF16), wq_ref[...], preferred_element_type=F32).astype(BF16)
    outs = []
    for h in range(heads):
        s = lax.dot_general(q[:, h * hd:(h + 1) * hd], k_ref[h], NT, preferred_element_type=F32) * scale
        e = jnp.exp(s - jnp.max(s, axis=-1, keepdims=True))
        p = (e / jnp.sum(e, axis=-1, keepdims=True)).astype(BF16)
        outs.append(jnp.dot(p, v_ref[h], preferred_element_type=F32).astype(BF16))
    o = jnp.concatenate(outs, axis=-1)
    x = x + jnp.dot(o, wo_ref[...], preferred_element_type=F32)
    o_ref[...] = _mlp_block(x, gm_ref, wu_ref, wd_ref, fg_ref, final_norm)


def _xattn_mlp_fused(x2, pre, g_attn, w_q, kh, vh, layer, w_o, g_mlp, w_up, w_down, final_g, *, seq_len, tt,
                     final_norm):
    m, d = x2.shape
    _, _, heads, mlen, hd = kh.shape
    per_seq = seq_len // tt
    kv = pl.BlockSpec((None, None, heads, mlen, hd), lambda i: (layer, i // per_seq, 0, 0, 0))
    tile = pl.BlockSpec((tt, d), lambda i: (i, 0))
    vec = lambda a: a.reshape(1, d)
    args = [x2, vec(g_attn), w_q, kh, vh, w_o, vec(g_mlp), w_up, w_down, vec(final_g)]
    in_specs = [tile, _resident((1, d)), _resident(w_q.shape), kv, kv, _resident(w_o.shape),
                _resident((1, d)), _resident_layer(w_up.shape, layer), _resident_layer(w_down.shape, layer),
                _resident((1, d))]
    if pre is not None:
        args = [pre[0], pre[1]] + args
        in_specs = [tile, _resident(pre[1].shape)] + in_specs
    return pl.pallas_call(
        functools.partial(_xattn_mlp_kernel, scale=hd ** -0.5, final_norm=final_norm, has_pre=pre is not None),
        grid=(m // tt,),
        in_specs=in_specs,
        out_specs=tile,
        out_shape=jax.ShapeDtypeStruct((m, d), F32),
        compiler_params=_params(1),
        name="xattn_mlp_fused",
    )(*args)


def _attn_kernel(q_ref, k_ref, v_ref, o_ref, *, scale):
    bb, tt, _ = q_ref.shape
    m, heads, hd = k_ref.shape[1], k_ref.shape[2], k_ref.shape[3]
    row_head = lax.broadcasted_iota(jnp.int32, (heads * tt, m * heads), 0) // tt
    col_head = lax.broadcasted_iota(jnp.int32, (heads * tt, m * heads), 1) % heads
    own = row_head == col_head
    for i in range(bb):
        k2 = k_ref[i].reshape(m * heads, hd).astype(BF16)
        v2 = v_ref[i].reshape(m * heads, hd).astype(BF16)
        q2 = jnp.concatenate([q_ref[i, :, h * hd:(h + 1) * hd] for h in range(heads)], axis=0).astype(BF16)
        s = lax.dot_general(q2, k2, NT, preferred_element_type=F32) * scale
        s = jnp.where(own, s, -jnp.inf)
        e = jnp.exp(s - jnp.max(s, axis=-1, keepdims=True))
        p = (e / jnp.sum(e, axis=-1, keepdims=True)).astype(BF16)
        o2 = jnp.dot(p, v2, preferred_element_type=F32)
        o_ref[i] = jnp.concatenate([o2[h * tt:(h + 1) * tt] for h in range(heads)], axis=-1).astype(o_ref.dtype)


def _attn(q, mem_k, mem_v, layer, *, bb, tt, out_dtype):
    b, t, d = q.shape
    _, _, m, heads, hd = mem_k.shape
    kv = pl.BlockSpec((None, bb, m, heads, hd), lambda i, j: (layer, i, 0, 0, 0))
    return pl.pallas_call(
        functools.partial(_attn_kernel, scale=hd ** -0.5),
        grid=(b // bb, t // tt),
        in_specs=[pl.BlockSpec((bb, tt, d), lambda i, j: (i, j, 0)), kv, kv],
        out_specs=pl.BlockSpec((bb, tt, d), lambda i, j: (i, j, 0)),
        out_shape=jax.ShapeDtypeStruct((b, t, d), out_dtype),
        compiler_params=_params(2),
        name="attn_core",
    )(q, mem_k, mem_v)


def _conv_kernel(x_ref, c0_ref, g_ref, win_ref, bin_ref, f_ref, bdw_ref, lg_ref, lb_ref, wout_ref, bout_ref,
                 o_ref, cs_ref, up_ref, *, lane_chunk, slab):
    bb, tt, d = x_ref.shape
    t = pl.program_id(1)
    lo = CONV_PAD - CONV_STATE
    x = x_ref[...].reshape(bb * tt, d)
    pre = jnp.dot(_rms(x, g_ref[...]).astype(BF16), win_ref[...], preferred_element_type=F32) + bin_ref[...]
    u = (pre[:, :d] * jax.nn.sigmoid(pre[:, d:])).reshape(bb, tt, d)

    @pl.when(t == 0)
    def _():
        up_ref[:, 0:lo, :] = jnp.zeros((bb, lo, d), F32)
        up_ref[:, CONV_PAD + tt:, :] = jnp.zeros((bb, SUBLANES, d), F32)
        up_ref[:, lo:CONV_PAD, :] = c0_ref[...]

    up_ref[:, CONV_PAD:CONV_PAD + tt, :] = u
    acts = []
    for i in range(bb):
        zs = []
        for c in range(d // lane_chunk):
            cs = slice(c * lane_chunk, (c + 1) * lane_chunk)
            acc = None
            for o in range(SUBLANES):
                g = None
                for j in range(CONV_WIDTH):
                    if (lo + j) % SUBLANES != o:
                        continue
                    base = lo + j - o
                    term = up_ref[i, base:base + tt + SUBLANES, cs] * f_ref[j:j + 1, cs]
                    g = term if g is None else g + term
                shifted = g[o:o + tt]
                acc = shifted if acc is None else acc + shifted
            zs.append(acc)
        z = jnp.concatenate(zs, axis=-1) + bdw_ref[...]
        zc = z - jnp.mean(z, axis=-1, keepdims=True)
        zn = zc * lax.rsqrt(jnp.mean(zc * zc, axis=-1, keepdims=True) + LN_EPS)
        zn = zn * lg_ref[...] + lb_ref[...]
        acts.append((zn * jax.nn.sigmoid(zn)).astype(BF16))
    act = acts[0] if bb == 1 else jnp.concatenate(acts, axis=0)
    out = x + jnp.dot(act, wout_ref[...], preferred_element_type=F32) + bout_ref[...]
    o_ref[...] = out.reshape(bb, tt, d)
    tail = up_ref[:, tt + lo:tt + CONV_PAD, :]
    if slab is None:
        cs_ref[...] = tail
    else:
        for other in range(cs_ref.shape[0]):
            cs_ref[other] = tail if other == slab else jnp.zeros_like(tail)
    up_ref[:, lo:CONV_PAD, :] = tail


def _conv_mixer(x, g, conv_all, ci, carry, p, *, bb, tt):
    b, t, d = x.shape
    vec = lambda a: a.reshape(1, -1)
    slab = pl.BlockSpec((None, bb, CONV_STATE, d), lambda i, j: (ci, i, 0, 0))
    args = [x, conv_all, vec(g), p['cv_w_in'][ci], vec(p['cv_b_in'][ci]), p['cv_w_dw'][ci], vec(p['cv_b_dw'][ci]),
            vec(p['cv_ln_g'][ci]), vec(p['cv_ln_b'][ci]), p['cv_w_out'][ci], vec(p['cv_b_out'][ci])]
    in_specs = [pl.BlockSpec((bb, tt, d), lambda i, j: (i, j, 0)), slab] + [_resident(a.shape) for a in args[2:]]
    aliases = {}
    if carry is None:
        body = functools.partial(_conv_kernel, lane_chunk=2 * LANES, slab=ci)
        slab_out = pl.BlockSpec((conv_all.shape[0], bb, CONV_STATE, d), lambda i, j: (0, i, 0, 0))
    else:
        args.append(carry)
        in_specs.append(pl.BlockSpec(memory_space=pl.ANY))
        aliases = {len(args) - 1: 1}
        body = _drop_ref(functools.partial(_conv_kernel, lane_chunk=2 * LANES, slab=None), len(args) - 1)
        slab_out = slab
    return pl.pallas_call(
        body,
        grid=(b // bb, t // tt),
        in_specs=in_specs,
        out_specs=[pl.BlockSpec((bb, tt, d), lambda i, j: (i, j, 0)), slab_out],
        out_shape=[jax.ShapeDtypeStruct((b, t, d), F32), jax.ShapeDtypeStruct(conv_all.shape, F32)],
        scratch_shapes=[pltpu.VMEM((bb, CONV_PAD + tt + SUBLANES, d), F32)],
        input_output_aliases=aliases,
        compiler_params=_params(2),
        name="conv_core",
    )(*args)


def _rwkv_proj_kernel(*refs, has_vfirst):
    (x_ref, sh0_ref, g_ref, mix_ref, wr_ref, wk_ref, wv_ref, w0_ref, w1_ref, w2_ref,
     a0_ref, a1_ref, a2_ref, g1_ref, g2_ref, kk_ref, ka_ref, bd_ref) = refs[:18]
    pos = 18
    if has_vfirst:
        vf_ref, v0_ref, v1_ref, v2_ref = refs[pos:pos + 4]
        pos += 4
    (r_o, lw_o, k_o, v_o, kkn_o, b_o, g_o, sh_o) = refs[pos:pos + 8]
    hs_ref = refs[pos + 8]

    bb, tt, d = x_ref.shape
    t = pl.program_id(1)
    h3 = _rms(x_ref[...], g_ref[...])

    @pl.when(t == 0)
    def _():
        hs_ref[...] = sh0_ref[...]

    m = bb * tt
    h = h3.reshape(m, d)
    first = jnp.broadcast_to(hs_ref[...], (bb, tt, d)).reshape(m, d)
    is_first = lax.broadcasted_iota(jnp.int32, (m, d), 0) % tt == 0
    prev = jnp.where(is_first, first, pltpu.roll(h, 1, axis=0))
    last = h3[:, tt - 1:tt, :]
    hs_ref[...] = last
    sh_o[...] = last

    xx = prev - h
    mix = mix_ref[...]
    xr, xw, xk, xv, xa, xg = ((h + xx * mix[i:i + 1, :]).astype(BF16) for i in range(6))

    r = jnp.dot(xr, wr_ref[...], preferred_element_type=F32)
    k = jnp.dot(xk, wk_ref[...], preferred_element_type=F32)
    v = jnp.dot(xv, wv_ref[...], preferred_element_type=F32)
    w_pre = w0_ref[...] + _bdot(jnp.tanh(jnp.dot(xw, w1_ref[...], preferred_element_type=F32)), w2_ref[...])
    lw = -math.exp(-0.5) * jax.nn.sigmoid(w_pre)
    if has_vfirst:
        gate = jax.nn.sigmoid(v0_ref[...] + _bdot(jnp.dot(xv, v1_ref[...], preferred_element_type=F32), v2_ref[...]))
        v = v + (vf_ref[...].reshape(m, d) - v) * gate
    a = jax.nn.sigmoid(a0_ref[...] + _bdot(jnp.dot(xa, a1_ref[...], preferred_element_type=F32), a2_ref[...]))
    g = _bdot(jax.nn.sigmoid(jnp.dot(xg, g1_ref[...], preferred_element_type=F32)), g2_ref[...])
    kk = k * kk_ref[...]
    nrm = jnp.maximum(jnp.sqrt(_segsum(kk * kk, bd_ref[...], 3)), L2_EPS)
    kk = kk / nrm
    k = k * (1.0 + (a - 1.0) * ka_ref[...])

    shp = (bb, tt, d)
    r_o[...] = r.reshape(shp)
    lw_o[...] = lw.reshape(shp)
    k_o[...] = k.reshape(shp)
    v_o[...] = v.reshape(shp)
    kkn_o[...] = kk.reshape(shp)
    b_o[...] = (kk * a).reshape(shp)
    g_o[...] = g.reshape(shp).astype(g_o.dtype)


def _rwkv_proj(x, shift0, g, p, ri, v_first, bd, *, bb, tt, gate_dtype):
    b, t, d = x.shape
    vec = lambda a: a.reshape(1, d)
    tile = pl.BlockSpec((bb, tt, d), lambda i, j: (i, j, 0))
    row = pl.BlockSpec((bb, 1, d), lambda i, j: (i, 0, 0))
    args = [x, shift0.reshape(b, 1, d), vec(g), p['rw_mix'][ri],
            p['rw_w_r'][ri], p['rw_w_k'][ri], p['rw_w_v'][ri],
            vec(p['rw_w0'][ri]), p['rw_w1'][ri], p['rw_w2'][ri],
            vec(p['rw_a0'][ri]), p['rw_a1'][ri], p['rw_a2'][ri],
            p['rw_g1'][ri], p['rw_g2'][ri], vec(p['rw_k_k'][ri]), vec(p['rw_k_a'][ri]), bd]
    in_specs = [tile, row] + [_resident(a.shape) for a in args[2:]]
    has_vfirst = v_first is not None
    if has_vfirst:
        vi = ri - 1
        extra = [v_first, vec(p['rw_v0'][vi]), p['rw_v1'][vi], p['rw_v2'][vi]]
        args += extra
        in_specs += [tile] + [_resident(a.shape) for a in extra[1:]]
    big = jax.ShapeDtypeStruct((b, t, d), F32)
    gate = jax.ShapeDtypeStruct((b, t, d), gate_dtype)
    outs = pl.pallas_call(
        functools.partial(_rwkv_proj_kernel, has_vfirst=has_vfirst),
        grid=(b // bb, t // tt),
        in_specs=in_specs,
        out_specs=[tile] * 7 + [row],
        out_shape=[big] * 6 + [gate, jax.ShapeDtypeStruct((b, 1, d), F32)],
        scratch_shapes=[pltpu.VMEM((bb, 1, d), F32)],
        compiler_params=_params(2),
        name="rwkv_proj",
    )(*args)
    return outs


def _scan_kernel(r_ref, lw_ref, k_ref, v_ref, kk_ref, b_ref, gate_ref, s0_ref, tri_ref, lg_ref, lb_ref, rk_ref,
                 z_ref, so_ref, s_scr, *, tc, C, slab):
    bb, _, d = r_ref.shape
    n = RWKV_HEAD
    n_pairs = d // LANES
    t = pl.program_id(1)

    @pl.when(t == 0)
    def _():
        for i in range(bb):
            for p in range(n_pairs):
                s_scr[i, p] = jnp.concatenate([s0_ref[i, 2 * p], s0_ref[i, 2 * p + 1]], axis=-1)

    def load(ref, i):
        x = ref[i]
        if tc < C:
            x = jnp.concatenate([x, jnp.zeros((C - tc, d), F32)], axis=0)
        return x

    lane_lo = lax.broadcasted_iota(jnp.int32, (1, LANES), 1) < n
    row = lax.broadcasted_iota(jnp.int32, (2 * C, 2 * C), 0)
    col = lax.broadcasted_iota(jnp.int32, (2 * C, 2 * C), 1)
    row_hi = row >= C
    col_hi = col >= C
    tr = jnp.where(row_hi, row - C, row)
    tj = jnp.where(col_hi, col - C, col)
    diag = row_hi == col_hi
    m_l = jnp.logical_and(diag, tj < tr)
    m_ak = jnp.logical_and(jnp.logical_not(diag), tj < tr)
    m_incl = tj <= tr
    eye = (row == col).astype(F32)
    rl_row = lax.broadcasted_iota(jnp.int32, (2 * C, LANES), 0) >= C
    rl_lane = lax.broadcasted_iota(jnp.int32, (2 * C, LANES), 1) >= n
    m_rl = rl_row == rl_lane

    def stack_masked(x):
        return jnp.concatenate([jnp.where(lane_lo, x, 0.0), jnp.where(lane_lo, 0.0, x)], axis=0).astype(BF16)

    def head_sum(x):
        lo = jnp.sum(jnp.where(lane_lo, x, 0.0), axis=-1, keepdims=True)
        hi = jnp.sum(jnp.where(lane_lo, 0.0, x), axis=-1, keepdims=True)
        return jnp.where(lane_lo, lo, hi)

    units = []
    for i in range(bb):
        r, lw, k, v, kk, b = (load(ref, i) for ref in (r_ref, lw_ref, k_ref, v_ref, kk_ref, b_ref))
        cl = _split_dot_lhs(tri_ref[...], lw, 3)
        cl_end = cl[C - 1:C, :]
        e_neg = jnp.exp(-cl)
        e_end = jnp.exp(cl_end - cl)
        a_hat = -kk * jnp.exp(cl - lw)
        r_hat = r * jnp.exp(cl)
        b_hat = b * e_neg
        k_hat = k * e_neg
        b_bar = b * e_end
        k_bar = k * e_end
        w_end = jnp.exp(cl_end)
        for p in range(n_pairs):
            sl = slice(p * LANES, (p + 1) * LANES)
            units.append(dict(i=i, p=p, sl=sl, vv=v[:, sl], a_hat=a_hat[:, sl], r_hat=r_hat[:, sl],
                              b_hat=b_hat[:, sl], k_hat=k_hat[:, sl], b_bar=b_bar[:, sl], k_bar=k_bar[:, sl],
                              w_end=w_end[:, sl], rk=r[:, sl] * k[:, sl]))

    for u in units:
        bh, kh = u['b_hat'], u['k_hat']
        u['s_old'] = s_scr[u['i'], u['p']]
        rhs = jnp.concatenate([u['s_old'], u['s_old'], jnp.where(lane_lo, bh, kh), jnp.where(lane_lo, kh, bh)],
                              axis=0).astype(BF16)
        lhs = jnp.concatenate([stack_masked(u['a_hat']), stack_masked(u['r_hat'])], axis=0)
        gp = lax.dot_general(lhs, rhs, NT, preferred_element_type=F32)
        u['p_a'], g_a = gp[:2 * C, :LANES], gp[:2 * C, LANES:]
        u['p_r'], g_r = gp[2 * C:, :LANES], gp[2 * C:, LANES:]
        l_mat = jnp.where(m_l, g_a, 0.0)
        u['t_mat'] = eye + l_mat
        u['l_pow'] = l_mat.astype(BF16)
        u['ak'] = jnp.where(m_ak, g_a, 0.0).astype(BF16)
        u['g_r'] = jnp.where(m_incl, g_r, 0.0).astype(BF16)

    n_factors = int(math.log2(C)) - 1
    for u in units:
        u['l_pow'] = _bdot(u['l_pow'], u['l_pow']).astype(BF16)
    for _ in range(n_factors - 1):
        for u in units:
            both = _bdot(jnp.concatenate([u['t_mat'].astype(BF16), u['l_pow']], axis=0), u['l_pow'])
            u['t_mat'] = u['t_mat'] + both[:2 * C]
            u['l_pow'] = both[2 * C:].astype(BF16)
    for u in units:
        u['t_mat'] = u['t_mat'] + _bdot(u['t_mat'], u['l_pow'])

    for u in units:
        vv2 = jnp.concatenate([u['vv'], u['vv']], axis=0)
        u['rhs_u'] = jnp.where(m_rl, u['p_a'] + _bdot(u['ak'], vv2), 0.0)
    for u in units:
        u_st = _bdot(u['t_mat'], u['rhs_u'])
        u['u_lo'], u['u_hi'] = u_st[:C], u_st[C:]
    for u in units:
        vv = u['vv']
        w_mix = jnp.concatenate([jnp.where(lane_lo, u['u_lo'], vv), jnp.where(lane_lo, vv, u['u_hi'])], axis=0)
        y_st = jnp.where(m_rl, u['p_r'] + _bdot(u['g_r'], w_mix), 0.0)
        y = y_st[:C] + y_st[C:]
        sl = u['sl']
        yc = y - head_sum(y) * (1.0 / n)
        yn = yc * lax.rsqrt(head_sum(yc * yc) * (1.0 / n) + LNX_EPS) * lg_ref[:, sl] + lb_ref[:, sl]
        bonus = head_sum(u['rk'] * rk_ref[:, sl]) * vv
        z = (yn[:tc] + bonus[:tc]) * gate_ref[u['i'], :, sl].astype(F32)
        z_ref[u['i'], :, sl] = z.astype(z_ref.dtype)
    for u in units:
        uv_t = jnp.concatenate([u['u_lo'] + u['u_hi'], u['vv']], axis=0).T
        bk = jnp.concatenate([u['b_bar'], u['k_bar']], axis=0)
        full = _bdot(uv_t, bk)
        s_new = u['s_old'] * u['w_end'] + jnp.where(lane_lo, full[:n], full[n:])
        s_scr[u['i'], u['p']] = s_new

    @pl.when(t == pl.num_programs(1) - 1)
    def _():
        dst = so_ref if slab is None else so_ref.at[slab]
        for i in range(bb):
            for p in range(n_pairs):
                s_fin = s_scr[i, p]
                dst[i, 2 * p] = s_fin[:, :n]
                dst[i, 2 * p + 1] = s_fin[:, n:]
        if slab is not None:
            for other in range(so_ref.shape[0]):
                if other != slab:
                    so_ref[other] = jnp.zeros(so_ref.shape[1:], F32)


def _rwkv_scan(r, lw, k, v, kk, b, gate, s_all, ri, carry, tri, lnx_g, lnx_b, r_k, *, bb, tc, out_dtype):
    bsz, t, d = r.shape
    n_pairs = d // LANES
    chunk = tri.shape[0]
    vec = lambda a: a.reshape(1, d)
    tile = pl.BlockSpec((bb, tc, d), lambda i, j: (i, j, 0))
    st = pl.BlockSpec((None, bb) + s_all.shape[2:], lambda i, j: (ri, i, 0, 0, 0))
    args = [r, lw, k, v, kk, b, gate, s_all, tri, vec(lnx_g), vec(lnx_b), vec(r_k)]
    in_specs = [tile] * 7 + [st, _resident(tri.shape)] + [_resident((1, d))] * 3
    aliases = {}
    if carry is None:
        body = functools.partial(_scan_kernel, tc=tc, C=chunk, slab=ri)
        st_out = pl.BlockSpec((s_all.shape[0], bb) + s_all.shape[2:], lambda i, j: (0, i, 0, 0, 0))
    else:
        args.append(carry)
        in_specs.append(pl.BlockSpec(memory_space=pl.ANY))
        aliases = {len(args) - 1: 1}
        body = _drop_ref(functools.partial(_scan_kernel, tc=tc, C=chunk, slab=None), len(args) - 1)
        st_out = st
    return pl.pallas_call(
        body,
        grid=(bsz // bb, t // tc),
        in_specs=in_specs,
        out_specs=[tile, st_out],
        out_shape=[jax.ShapeDtypeStruct((bsz, t, d), out_dtype), jax.ShapeDtypeStruct(s_all.shape, F32)],
        scratch_shapes=[pltpu.VMEM((bb, n_pairs, RWKV_HEAD, LANES), F32)],
        input_output_aliases=aliases,
        compiler_params=_params(2),
        name="rwkv_scan",
    )(*args)


def _trunk(x, mem_k, mem_v, conv_state, shift_state, wkv_state, p, consts, cfg, mem_heads=None):
    bsz, t, d = x.shape
    m = bsz * t
    depth = p['norm_g'].shape[0]
    bd = consts
    ar = jnp.arange(cfg['scan_chunk'])
    tri = (ar[None, :] <= ar[:, None]).astype(BF16)
    x2 = x.reshape(m, d)
    conv_new, wkv_new, shift_new = None, None, []
    v_first = None
    for layer in range(depth):
        g = p['norm_g'][layer]
        pending = None
        if layer % 2 == 0:
            ci = layer // 2
            x3, conv_new = _conv_mixer(x2.reshape(bsz, t, d), g[0], conv_state, ci, conv_new, p,
                                       bb=cfg['conv_bb'], tt=cfg['conv_tt'])
            x2 = x3.reshape(m, d)
        else:
            ri = layer // 2
            r, lw, k, v, kk, b, gt, sh = _rwkv_proj(x2.reshape(bsz, t, d), shift_state[ri], g[0], p, ri, v_first, bd,
                                                    bb=cfg['proj_bb'], tt=cfg['proj_tt'],
                                                    gate_dtype=cfg['gate_dtype'])
            if v_first is None:
                v_first = v
            z, wkv_new = _rwkv_scan(r, lw, k, v, kk, b, gt, wkv_state, ri, wkv_new, tri, p['rw_lnx_g'][ri],
                                    p['rw_lnx_b'][ri], p['rw_r_k'][ri], bb=cfg['scan_bb'], tc=cfg['scan_tc'],
                                    out_dtype=cfg['gate_dtype'])
            shift_new.append(sh.reshape(bsz, d))
            pending = (z.reshape(m, d), p['rw_w_o'][ri])
        last = layer == depth - 1
        if mem_heads is not None:
            x2 = _xattn_mlp_fused(x2, pending, g[1], p['xa_w_q'][layer], mem_heads[0], mem_heads[1], layer,
                                  p['xa_w_o'][layer], g[2], p['mlp_w_up'], p['mlp_w_down'],
                                  p['final_g'], seq_len=t, tt=cfg['attn_tt'], final_norm=last)
        else:
            if pending is not None:
                x2 = _matmul_res(pending[0], pending[1], x2)
            q = _norm_matmul(x2, g[1], p['xa_w_q'][layer], out_dtype=cfg['attn_dtype'])
            o = _attn(q.reshape(bsz, t, d), mem_k, mem_v, layer, bb=cfg['attn_bb'], tt=cfg['attn_tt'],
                      out_dtype=cfg['attn_dtype'])
            x2 = _matmul_res(o.reshape(m, d), p['xa_w_o'][layer], x2)
            x2 = _mlp(x2, g[2], p['mlp_w_up'], p['mlp_w_down'], layer, p['final_g'], final_norm=last)
    return x2.reshape(bsz, t, d), conv_new, jnp.stack(shift_new), wkv_new


PROMPT_CFG = dict(conv_bb=1, conv_tt=256, proj_bb=1, proj_tt=256, gate_dtype=BF16, scan_bb=2, scan_tc=SCAN_CHUNK,
                  scan_chunk=SCAN_CHUNK, attn_bb=1, attn_tt=512, attn_dtype=BF16)
SAMPLE_CFG = dict(conv_bb=16, conv_tt=8, proj_bb=16, proj_tt=8, gate_dtype=F32, scan_bb=8, scan_tc=8,
                  scan_chunk=8, attn_bb=8, attn_tt=8, attn_dtype=F32)


def kernel(x_prompt, x_sample, mem_prompt, cache_mem_k, cache_mem_v, state_conv, state_shift, state_wkv, norm_g, final_g, cv_w_in, cv_b_in, cv_w_dw, cv_b_dw, cv_ln_g, cv_ln_b, cv_w_out, cv_b_out, rw_mix, rw_w_r, rw_w_k, rw_w_v, rw_w_o, rw_w0, rw_w1, rw_w2, rw_a0, rw_a1, rw_a2, rw_v0, rw_v1, rw_v2, rw_g1, rw_g2, rw_k_k, rw_k_a, rw_r_k, rw_lnx_g, rw_lnx_b, xa_w_q, xa_w_kv, xa_w_o, mlp_w_up, mlp_w_down):
    bf = lambda w: w.astype(BF16)
    p = {
        'norm_g': norm_g, 'final_g': final_g,
        'cv_w_in': bf(cv_w_in), 'cv_b_in': cv_b_in, 'cv_w_dw': cv_w_dw, 'cv_b_dw': cv_b_dw,
        'cv_ln_g': cv_ln_g, 'cv_ln_b': cv_ln_b, 'cv_w_out': bf(cv_w_out), 'cv_b_out': cv_b_out,
        'rw_mix': rw_mix, 'rw_w_r': bf(rw_w_r), 'rw_w_k': bf(rw_w_k), 'rw_w_v': bf(rw_w_v), 'rw_w_o': bf(rw_w_o),
        'rw_w0': rw_w0, 'rw_w1': bf(rw_w1), 'rw_w2': bf(rw_w2), 'rw_a0': rw_a0, 'rw_a1': bf(rw_a1),
        'rw_a2': bf(rw_a2), 'rw_v0': rw_v0, 'rw_v1': bf(rw_v1), 'rw_v2': bf(rw_v2), 'rw_g1': bf(rw_g1),
        'rw_g2': bf(rw_g2), 'rw_k_k': rw_k_k, 'rw_k_a': rw_k_a, 'rw_r_k': rw_r_k,
        'rw_lnx_g': rw_lnx_g, 'rw_lnx_b': rw_lnx_b,
        'xa_w_q': bf(xa_w_q), 'xa_w_o': bf(xa_w_o), 'mlp_w_up': bf(mlp_w_up), 'mlp_w_down': bf(mlp_w_down),
    }
    depth = norm_g.shape[0]
    bp, mlen, d = mem_prompt.shape
    n_conv, n_rwkv = state_conv.shape[0], state_wkv.shape[0]
    heads, n = state_wkv.shape[2], state_wkv.shape[3]
    hd = d // XA_HEADS

    idx = jnp.arange(SEG_TILE) // RWKV_HEAD
    bd = (idx[:, None] == idx[None, :]).astype(BF16)
    consts = bd

    mem_k_prompt, mem_v_prompt, mem_kh, mem_vh = _memkv(mem_prompt, bf(xa_w_kv))

    conv0 = jnp.zeros((n_conv, bp, CONV_STATE, d), F32)
    shift0 = jnp.zeros((n_rwkv, bp, d), F32)
    wkv0 = jnp.zeros((n_rwkv, bp, heads, n, n), F32)
    y_prompt, conv_prompt, shift_prompt, wkv_prompt = _trunk(
        x_prompt, mem_k_prompt, mem_v_prompt, conv0, shift0, wkv0, p, consts, PROMPT_CFG,
        mem_heads=(mem_kh, mem_vh))
    y_sample, conv_sample, shift_sample, wkv_sample = _trunk(
        x_sample, cache_mem_k, cache_mem_v, state_conv, state_shift, state_wkv, p, consts, SAMPLE_CFG)
    return (y_prompt, y_sample, mem_k_prompt, mem_v_prompt, conv_prompt, shift_prompt, wkv_prompt,
            conv_sample, shift_sample, wkv_sample)
```
